```python
import jax, jax.numpy as jnp
from jax import lax
import numpy as np

D_MODEL = 1024
BATCH = 8
SEQ = 2048
DEPTH = 4

CTX_LEN = 256
GRID_W = 64
N_MIXERS = 3
N_HEADS = 16
N_KV_HEADS = 4
HEAD_DIM = D_MODEL // N_HEADS
GROUP = N_HEADS // N_KV_HEADS
ATTN_DIM = N_HEADS * HEAD_DIM
KV_DIM = N_KV_HEADS * HEAD_DIM
QKV_DIM = ATTN_DIM + 2 * KV_DIM
ROPE_BASE = 10000.0
ROPE_FREQS = HEAD_DIM // 4
Q_BLOCK = 128
WINDOW = 128
HY_SHORT = 3
HY_BANDS = 16
HY_EMB = 1 + 2 * HY_BANDS
HY_FILTER_WIDTH = 64
HY_DECAY_MIN = 3.07
HY_DECAY_MAX = 15.35
DENSE_FF = 2816
N_EXPERTS = 8
TOP_K = 2
EXPERT_FF = 3584
NORM_EPS = 1e-6
N_GLOBAL = (DEPTH + 2) // 3
N_WINDOW = (DEPTH + 1) // 3
N_HYENA = DEPTH // 3
N_DENSE = (DEPTH + 1) // 2
N_MOE = DEPTH // 2

kernel_name = 'hybrid_interleaved_diffusion_trunk'


def _rmsnorm(x, g):
    xf = x.astype(jnp.float32)
    y = xf * lax.rsqrt(jnp.mean(xf * xf, axis=-1, keepdims=True) + NORM_EPS)
    return (y * g.astype(jnp.float32)).astype(x.dtype)


def _modulate(x, g, shift, scale):
    return _rmsnorm(x, g) * (1.0 + scale) + shift


def _rope_tables(L):
    rows = L // GRID_W
    row = jnp.repeat(jnp.arange(rows, dtype=jnp.float32), GRID_W)
    col = jnp.tile(jnp.arange(GRID_W, dtype=jnp.float32), rows)
    inv = ROPE_BASE ** (-jnp.arange(ROPE_FREQS, dtype=jnp.float32) / ROPE_FREQS)
    ang = jnp.stack([row[:, None] * inv, col[:, None] * inv], axis=1)
    return jnp.cos(ang), jnp.sin(ang)


def _apply_rope(x, cos, sin):
    shp = x.shape
    xs = x.astype(jnp.float32).reshape(shp[:-1] + (2, 2, ROPE_FREQS))
    x1, x2 = xs[..., 0, :], xs[..., 1, :]
    c, s = cos[:, None], sin[:, None]
    out = jnp.stack([x1 * c - x2 * s, x2 * c + x1 * s], axis=-2)
    return out.reshape(shp).astype(x.dtype)


def _qkv(h, w_in, q_g, k_g, rope):
    B, L, _ = h.shape
    p = h @ w_in
    q = p[..., :ATTN_DIM].reshape(B, L, N_HEADS, HEAD_DIM)
    k = p[..., ATTN_DIM:ATTN_DIM + KV_DIM].reshape(B, L, N_KV_HEADS, HEAD_DIM)
    v = p[..., ATTN_DIM + KV_DIM:].reshape(B, L, N_KV_HEADS, HEAD_DIM)
    q = _rmsnorm(q, q_g)
    k = _rmsnorm(k, k_g)
    if rope is not None:
        q = _apply_rope(q, *rope)
        k = _apply_rope(k, *rope)
    return q.reshape(B, L, N_KV_HEADS, GROUP, HEAD_DIM), k, v


def _attend(q, k, v, mask, sink):
    s = jnp.einsum('bqhgd,bshd->bhgqs', q, k).astype(jnp.float32) * (HEAD_DIM ** -0.5)
    if mask is not None:
        s = jnp.where(mask, s, -1e30)
    if sink is not None:
        sink_col = jnp.broadcast_to(sink.astype(jnp.float32)[None, :, :, None, None], s.shape[:-1] + (1,))
        p = jax.nn.softmax(jnp.concatenate([s, sink_col], axis=-1), axis=-1)[..., :-1]
    else:
        p = jax.nn.softmax(s, axis=-1)
    return jnp.einsum('bhgqs,bshd->bqhgd', p.astype(v.dtype), v)


def _global_attention(h, hc, w_in, w_out, q_g, k_g, rope, need_ctx):
    B, L, _ = h.shape
    Lc = hc.shape[1]
    q, k, v = _qkv(h, w_in, q_g, k_g, rope)
    qc, kc, vc = _qkv(hc, w_in, q_g, k_g, None)
    k_all = jnp.concatenate([k, kc], axis=1)
    v_all = jnp.concatenate([v, vc], axis=1)
    nb = L // Q_BLOCK
    q_blocks = jnp.swapaxes(q.reshape(B, nb, Q_BLOCK, N_KV_HEADS, GROUP, HEAD_DIM), 0, 1)
    o = lax.map(lambda qb: _attend(qb, k_all, v_all, None, None), q_blocks)
    y = jnp.swapaxes(o, 0, 1).reshape(B, L, ATTN_DIM) @ w_out
    yc = _attend(qc, kc, vc, None, None).reshape(B, Lc, ATTN_DIM) @ w_out if need_ctx else None
    return y, yc


def _window_attention(h, hc, w_in, w_out, q_g, k_g, sink, rope, need_ctx):
    B, L, _ = h.shape
    Lc = hc.shape[1]
    sink = sink.reshape(N_KV_HEADS, GROUP)
    q, k, v = _qkv(h, w_in, q_g, k_g, rope)
    qc, kc, vc = _qkv(hc, w_in, q_g, k_g, None)
    pad = ((0, 0), (WINDOW, WINDOW), (0, 0), (0, 0))
    k_pad, v_pad = jnp.pad(k, pad), jnp.pad(v, pad)
    band = Q_BLOCK + 2 * WINDOW
    ctx_mask = jnp.ones((Q_BLOCK, Lc), dtype=bool)

    def block(n):
        start = n * Q_BLOCK
        qb = lax.dynamic_slice_in_dim(q, start, Q_BLOCK, axis=1)
        kb = lax.dynamic_slice_in_dim(k_pad, start, band, axis=1)
        vb = lax.dynamic_slice_in_dim(v_pad, start, band, axis=1)
        qi = start + jnp.arange(Q_BLOCK)
        kj = start - WINDOW + jnp.arange(band)
        mask = (jnp.abs(qi[:, None] - kj[None, :]) <= WINDOW) & (kj >= 0)[None, :] & (kj < L)[None, :]
        return _attend(qb, jnp.concatenate([kb, kc], axis=1), jnp.concatenate([vb, vc], axis=1),
                       jnp.concatenate([mask, ctx_mask], axis=1), sink)

    o = lax.map(block, jnp.arange(L // Q_BLOCK))
    y = jnp.swapaxes(o, 0, 1).reshape(B, L, ATTN_DIM) @ w_out
    yc = _attend(qc, kc, vc, None, sink).reshape(B, Lc, ATTN_DIM) @ w_out if need_ctx else None
    return y, yc


def _hyena_filter(L, w1, b1, w2, b2, w3, sin_freq, log_decay):
    f32 = jnp.float32
    t = jnp.arange(L, dtype=f32)
    t_unit = t / max(L - 1, 1)
    phase = 2.0 * np.pi * t / L
    bands = jnp.linspace(1e-4, HY_BANDS - 1, HY_BANDS, dtype=f32)
    ang = phase[:, None] * bands[None, :]
    z = jnp.concatenate([t_unit[:, None], jnp.cos(ang), -jnp.sin(ang)], axis=-1)
    a = jnp.sin(sin_freq[0] * (z @ w1 + b1))
    a = jnp.sin(sin_freq[1] * (a @ w2 + b2))
    h = (a @ w3).astype(f32).reshape(L, 2, D_MODEL)
    h = h * jnp.exp(-t_unit[:, None, None] * jnp.exp(log_decay.astype(f32))[None])
    taps = jnp.concatenate([h[:, 0], jnp.zeros((1, D_MODEL), f32), h[:0:-1, 1]], axis=0)
    return taps / jnp.sum(jnp.abs(taps), axis=0, keepdims=True)


def _hyena_seq(u, w_in, b_in, conv_w, conv_b, filt, skip, w_out, b_out):
    L = u.shape[1]
    p = u @ w_in + b_in
    C = p.shape[-1]
    p = lax.conv_general_dilated(p, conv_w[:, None, :].astype(p.dtype), window_strides=(1,),
                                 padding=((HY_SHORT // 2, HY_SHORT // 2),),
                                 dimension_numbers=('NWC', 'WIO', 'NWC'),
                                 feature_group_count=C) + conv_b
    x0, x1, v = jnp.split(p, 3, axis=-1)
    v = (x1 * v).astype(jnp.float32)
    taps = _hyena_filter(L, *filt)
    spec = jnp.fft.rfft(v, n=2 * L, axis=1) * jnp.fft.rfft(taps, axis=0)[None]
    v = jnp.fft.irfft(spec, n=2 * L, axis=1)[:, :L] + skip.astype(jnp.float32) * v
    y = x0 * v.astype(u.dtype)
    return y @ w_out + b_out


def _hyena(h, hc, w_in, b_in, conv_w, conv_b, filt, skip, w_out, b_out, need_ctx):
    y = _hyena_seq(h, w_in, b_in, conv_w, conv_b, filt, skip, w_out, b_out)
    yc = _hyena_seq(hc, w_in, b_in, conv_w, conv_b, filt, skip, w_out, b_out) if need_ctx else None
    return y, yc


def _swiglu(h, w_gate, w_up, w_down):
    return (jax.nn.silu(h @ w_gate) * (h @ w_up)) @ w_down


def _moe(h, router, w_gate, w_up, w_down):
    logits = (h @ router).astype(jnp.float32)
    top_val, top_idx = lax.top_k(logits, TOP_K)
    top_w = jax.nn.softmax(top_val, axis=-1)
    combine = jnp.sum(jax.nn.one_hot(top_idx, N_EXPERTS, dtype=jnp.float32) * top_w[..., None], axis=-2)
    combine = combine.astype(h.dtype)
    out = jnp.zeros_like(h)
    for e in range(N_EXPERTS):
        out = out + combine[..., e:e + 1] * _swiglu(h, w_gate[e], w_up[e], w_down[e])
    return out


def setup_inputs(seed: int = 0) -> dict:
    key = jax.random.key(seed)
    keys = iter(jax.random.split(key, 64))
    D = D_MODEL

    def nrm(shape, scale):
        return jax.random.normal(next(keys), shape, jnp.float32) * scale

    decay_base = jnp.log(jnp.linspace(HY_DECAY_MIN, HY_DECAY_MAX, D, dtype=jnp.float32))
    return {
        'x': nrm((BATCH, SEQ, D), 1.0),
        'c': nrm((BATCH, D), 1.0),
        'ctx': nrm((BATCH, CTX_LEN, D), 1.0),
        'c_ctx': nrm((D,), 1.0),
        'mod_w': nrm((DEPTH, D, 6 * D), 0.5 * D ** -0.5),
        'mod_b': nrm((DEPTH, 6 * D), 0.02),
        'norm_g': 1.0 + nrm((DEPTH, 2, D), 0.02),
        'gattn_w_in': nrm((N_GLOBAL, D, QKV_DIM), D ** -0.5),
        'gattn_w_out': nrm((N_GLOBAL, ATTN_DIM, D), ATTN_DIM ** -0.5),
        'gattn_q_norm': 1.0 + nrm((N_GLOBAL, HEAD_DIM), 0.02),
        'gattn_k_norm': 1.0 + nrm((N_GLOBAL, HEAD_DIM), 0.02),
        'wattn_w_in': nrm((N_WINDOW, D, QKV_DIM), D ** -0.5),
        'wattn_w_out': nrm((N_WINDOW, ATTN_DIM, D), ATTN_DIM ** -0.5),
        'wattn_q_norm': 1.0 + nrm((N_WINDOW, HEAD_DIM), 0.02),
        'wattn_k_norm': 1.0 + nrm((N_WINDOW, HEAD_DIM), 0.02),
        'wattn_sink': nrm((N_WINDOW, N_HEADS), 0.5),
        'hy_w_in': nrm((N_HYENA, D, 3 * D), D ** -0.5),
        'hy_b_in': nrm((N_HYENA, 3 * D), 0.02),
        'hy_conv_w': nrm((N_HYENA, HY_SHORT, 3 * D), HY_SHORT ** -0.5),
        'hy_conv_b': nrm((N_HYENA, 3 * D), 0.02),
        'hy_ffn_w1': nrm((N_HYENA, HY_EMB, HY_FILTER_WIDTH), HY_EMB ** -0.5),
        'hy_ffn_b1': nrm((N_HYENA, HY_FILTER_WIDTH), 0.1),
        'hy_ffn_w2': nrm((N_HYENA, HY_FILTER_WIDTH, HY_FILTER_WIDTH), HY_FILTER_WIDTH ** -0.5),
        'hy_ffn_b2': nrm((N_HYENA, HY_FILTER_WIDTH), 0.1),
        'hy_ffn_w3': nrm((N_HYENA, HY_FILTER_WIDTH, 2 * D), HY_FILTER_WIDTH ** -0.5),
        'hy_sin_freq': 1.0 + nrm((N_HYENA, 2, HY_FILTER_WIDTH), 0.1),
        'hy_log_decay': decay_base + nrm((N_HYENA, 2, D), 0.1),
        'hy_skip': nrm((N_HYENA, D), 1.0),
        'hy_w_out': nrm((N_HYENA, D, D), D ** -0.5),
        'hy_b_out': nrm((N_HYENA, D), 0.02),
        'ffn_w_gate': nrm((N_DENSE, D, DENSE_FF), D ** -0.5),
        'ffn_w_up': nrm((N_DENSE, D, DENSE_FF), D ** -0.5),
        'ffn_w_down': nrm((N_DENSE, DENSE_FF, D), DENSE_FF ** -0.5),
        'moe_router': nrm((N_MOE, D, N_EXPERTS), D ** -0.5),
        'moe_w_gate': nrm((N_MOE, N_EXPERTS, D, EXPERT_FF), D ** -0.5),
        'moe_w_up': nrm((N_MOE, N_EXPERTS, D, EXPERT_FF), D ** -0.5),
        'moe_w_down': nrm((N_MOE, N_EXPERTS, EXPERT_FF, D), EXPERT_FF ** -0.5),
        'final_norm_g': 1.0 + nrm((D,), 0.02),
    }


def reference(x, c, ctx, c_ctx, mod_w, mod_b, norm_g,
              gattn_w_in, gattn_w_out, gattn_q_norm, gattn_k_norm,
              wattn_w_in, wattn_w_out, wattn_q_norm, wattn_k_norm, wattn_sink,
              hy_w_in, hy_b_in, hy_conv_w, hy_conv_b, hy_ffn_w1, hy_ffn_b1, hy_ffn_w2, hy_ffn_b2,
              hy_ffn_w3, hy_sin_freq, hy_log_decay, hy_skip, hy_w_out, hy_b_out,
              ffn_w_gate, ffn_w_up, ffn_w_down,
              moe_router, moe_w_gate, moe_w_up, moe_w_down,
              final_norm_g):
    L = x.shape[1]
    rope = _rope_tables(L)
    for i in range(DEPTH):
        last = i == DEPTH - 1
        mod = jax.nn.silu(c) @ mod_w[i] + mod_b[i]
        mod_c = jax.nn.silu(c_ctx) @ mod_w[i] + mod_b[i]
        sh_a, sc_a, gt_a, sh_f, sc_f, gt_f = [m[:, None] for m in jnp.split(mod, 6, axis=-1)]
        csh_a, csc_a, cgt_a, csh_f, csc_f, cgt_f = jnp.split(mod_c, 6, axis=-1)

        h = _modulate(x, norm_g[i, 0], sh_a, sc_a)
        hc = _modulate(ctx, norm_g[i, 0], csh_a, csc_a)
        kind, j = i % N_MIXERS, i // N_MIXERS
        if kind == 0:
            y, yc = _global_attention(h, hc, gattn_w_in[j], gattn_w_out[j], gattn_q_norm[j],
                                      gattn_k_norm[j], rope, not last)
        elif kind == 1:
            y, yc = _window_attention(h, hc, wattn_w_in[j], wattn_w_out[j], wattn_q_norm[j],
                                      wattn_k_norm[j], wattn_sink[j], rope, not last)
        else:
            filt = (hy_ffn_w1[j], hy_ffn_b1[j], hy_ffn_w2[j], hy_ffn_b2[j], hy_ffn_w3[j],
                    hy_sin_freq[j], hy_log_decay[j])
            y, yc = _hyena(h, hc, hy_w_in[j], hy_b_in[j], hy_conv_w[j], hy_conv_b[j], filt,
                           hy_skip[j], hy_w_out[j], hy_b_out[j], not last)
        x = x + gt_a * y
        if not last:
            ctx = ctx + cgt_a * yc

        h = _modulate(x, norm_g[i, 1], sh_f, sc_f)
        k_ff = i // 2
        if i % 2 == 0:
            x = x + gt_f * _swiglu(h, ffn_w_gate[k_ff], ffn_w_up[k_ff], ffn_w_down[k_ff])
            if not last:
                hc = _modulate(ctx, norm_g[i, 1], csh_f, csc_f)
                ctx = ctx + cgt_f * _swiglu(hc, ffn_w_gate[k_ff], ffn_w_up[k_ff], ffn_w_down[k_ff])
        else:
            x = x + gt_f * _moe(h, moe_router[k_ff], moe_w_gate[k_ff], moe_w_up[k_ff], moe_w_down[k_ff])
            if not last:
                hc = _modulate(ctx, norm_g[i, 1], csh_f, csc_f)
                ctx = ctx + cgt_f * _moe(hc, moe_router[k_ff], moe_w_gate[k_ff], moe_w_up[k_ff],
                                         moe_w_down[k_ff])
    return _rmsnorm(x, final_norm_g)
```

```python
import functools
import math
from typing import NamedTuple

import numpy as np
import jax
import jax.numpy as jnp
from jax import lax
from jax.experimental import pallas as pl
from jax.experimental.pallas import tpu as pltpu

F32, BF16, I32 = jnp.float32, jnp.bfloat16, jnp.int32

HEAD_DIM = 64
GROUP = 4
ROPE_BASE = 10000.0
ROPE_FREQS = HEAD_DIM // 4
HY_BANDS = 16
NORM_EPS = 1e-6
N_EXPERTS = 8
LANES = 128
VMEM_LIMIT = 56 * 1024 * 1024


class Cfg(NamedTuple):
    B: int = 8
    L: int = 2048
    LC: int = 256
    D: int = 1024
    GRID_W: int = 64
    WINDOW: int = 128
    DEPTH: int = 4
    TM: int = 512
    TM_FFN: int = 1024
    TF: int = 256
    TQ: int = 128
    TS: int = 512
    TSG: int = 256
    TB: int = 256
    TFE: int = 512
    TC: int = 256
    TMF: int = 512

    @property
    def ML(self):
        return self.B * self.L

    @property
    def MC(self):
        return self.B * self.LC

    @property
    def MT(self):
        return self.ML + self.MC


def _cparams(sem):
    return pltpu.CompilerParams(dimension_semantics=sem, vmem_limit_bytes=VMEM_LIMIT)


def _dot(a, b):
    return jnp.dot(a, b, preferred_element_type=F32)


def _dot_nt(a, b):
    return lax.dot_general(a, b, (((1,), (1,)), ((), ())), preferred_element_type=F32)


def _split(x):
    hi = x.astype(BF16)
    return hi, (x - hi.astype(F32)).astype(BF16)


def _dot3(a, b):
    ah, al = _split(a)
    bh, bl = _split(b)
    return _dot(ah, bh) + (_dot(al, bh) + _dot(ah, bl))


def _silu(a):
    return a * jax.nn.sigmoid(a)


def _modnorm(x, g, shift, scale):
    y = x * lax.rsqrt(jnp.mean(x * x, axis=-1, keepdims=True) + NORM_EPS)
    return (y * g) * (1.0 + scale) + shift


def _mod_row(cfg, i, tm):
    return jnp.minimum((i * tm) // cfg.L, cfg.B)


def _mod_spec(cfg, tm, k):
    D = cfg.D
    return pl.BlockSpec((1, 1, D), lambda i, *_: (_mod_row(cfg, i, tm) * 6 + k, 0, 0))


def _mod_kernel(c_ref, w_ref, b_ref, o_ref):
    o_ref[0] = _dot3(_silu(c_ref[...]), w_ref[0]) + b_ref[0]


def _mod_vectors(cfg, c, c_ctx, mod_w, mod_b):
    D, depth = cfg.D, cfg.DEPTH
    tn = 3 * D // 2
    rows = jnp.zeros((16, D), F32).at[:cfg.B].set(c).at[cfg.B].set(c_ctx)
    out = pl.pallas_call(
        _mod_kernel,
        grid=(depth, 6 * D // tn),
        in_specs=[pl.BlockSpec((16, D), lambda l, j: (0, 0)),
                  pl.BlockSpec((1, D, tn), lambda l, j: (l, 0, j)),
                  pl.BlockSpec((1, 1, tn), lambda l, j: (l, 0, j))],
        out_specs=pl.BlockSpec((1, 16, tn), lambda l, j: (l, 0, j)),
        out_shape=jax.ShapeDtypeStruct((depth, 16, 6 * D), F32),
        compiler_params=_cparams(("arbitrary", "arbitrary")),
        name="mod_vectors",
    )(rows, mod_w, mod_b.reshape(depth, 1, 6 * D))
    return out.reshape(depth, 16 * 6, 1, D)


def _qkv_kernel(x_ref, sh_ref, sc_ref, g_ref, w_ref, qg_ref, kg_ref, cos_ref, sin_ref, bd_ref,
                q_ref, k_ref, v_ref, *, attn_dim, kv_dim):
    h = _modnorm(x_ref[...], g_ref[...], sh_ref[0], sc_ref[0])
    p = _dot(h.astype(BF16), w_ref[...])
    tm = p.shape[0]
    upper = (lax.broadcasted_iota(I32, (tm, LANES), 1) & ROPE_FREQS) != 0
    cosv, sinv, bd = cos_ref[...], sin_ref[...], bd_ref[...]

    def norm_rope(xc, gain):
        hi, lo = _split(xc * xc)
        ss = _dot(hi, bd) + _dot(lo, bd)
        y = (xc * lax.rsqrt(ss * (1.0 / HEAD_DIM) + NORM_EPS)) * gain
        partner = jnp.where(upper, pltpu.roll(y, ROPE_FREQS, 1), pltpu.roll(y, LANES - ROPE_FREQS, 1))
        return y * cosv + partner * sinv

    for c in range(attn_dim // LANES):
        sl = slice(c * LANES, (c + 1) * LANES)
        q_ref[:, sl] = norm_rope(p[:, sl], qg_ref[...]).astype(BF16)
    for c in range(kv_dim // LANES):
        sl = slice(c * LANES, (c + 1) * LANES)
        k_ref[:, sl] = norm_rope(p[:, attn_dim + c * LANES: attn_dim + (c + 1) * LANES], kg_ref[...]).astype(BF16)
    v_ref[...] = p[:, attn_dim + kv_dim:].astype(BF16)


def _qkv_proj(cfg, x, mod3, g, w, qg, kg, cos_t, sin_t, bd):
    D, tm = cfg.D, cfg.TM
    kv = D // GROUP
    n_lat, per_seq = cfg.ML // tm, cfg.L // tm
    rope_map = lambda i: (jnp.where(i < n_lat, i % per_seq, per_seq), 0)
    full = lambda shp: pl.BlockSpec(shp, lambda i: (0,) * len(shp))
    return pl.pallas_call(
        functools.partial(_qkv_kernel, attn_dim=D, kv_dim=kv),
        grid=(cfg.MT // tm,),
        in_specs=[pl.BlockSpec((tm, D), lambda i: (i, 0)),
                  _mod_spec(cfg, tm, 0), _mod_spec(cfg, tm, 1),
                  full((1, D)), full((D, D + 2 * kv)), full((1, LANES)), full((1, LANES)),
                  pl.BlockSpec((tm, LANES), rope_map), pl.BlockSpec((tm, LANES), rope_map),
                  full((LANES, LANES))],
        out_specs=[pl.BlockSpec((tm, D), lambda i: (i, 0)),
                   pl.BlockSpec((tm, kv), lambda i: (i, 0)),
                   pl.BlockSpec((tm, kv), lambda i: (i, 0))],
        out_shape=[jax.ShapeDtypeStruct((cfg.MT, D), BF16),
                   jax.ShapeDtypeStruct((cfg.MT, kv), BF16),
                   jax.ShapeDtypeStruct((cfg.MT, kv), BF16)],
        compiler_params=_cparams(("parallel",)),
        name="qkv_proj",
    )(x, mod3, mod3, g, w, qg, kg, cos_t, sin_t, bd)


def _attn_kernel(sink_ref, q_ref, *refs, mode, tq, seq, window, use_sink):
    if mode == "ctx":
        kc_ref, vc_ref, o_ref = refs
    else:
        kl_ref, vl_ref, kc_ref, vc_ref, o_ref = refs
    hp, qi = pl.program_id(1), pl.program_id(2)
    rows = GROUP * tq
    low = lax.broadcasted_iota(I32, (tq, LANES), 1) < HEAD_DIM
    kc, vc = kc_ref[...], vc_ref[...]
    if mode == "global":
        kl, vl = kl_ref[...], vl_ref[...]
    elif mode == "window":
        span = tq + 2 * window
        start = pl.multiple_of(jnp.clip(qi * tq - window, 0, seq - span), math.gcd(tq, window))
        kl, vl = kl_ref[pl.ds(start, span), :], vl_ref[pl.ds(start, span), :]
        qpos = qi * tq + (lax.broadcasted_iota(I32, (rows, span), 0) & (tq - 1))
        kpos = start + lax.broadcasted_iota(I32, (rows, span), 1)
        in_window = jnp.abs(qpos - kpos) <= window
    outs = []
    for s in range(2):
        keep = low if s == 0 else jnp.logical_not(low)
        qs = jnp.concatenate(
            [jnp.where(keep, q_ref[:, c * LANES:(c + 1) * LANES], jnp.zeros((), BF16)) for c in range(GROUP)],
            axis=0)
        s_c = _dot_nt(qs, kc)
        m = jnp.max(s_c, axis=-1, keepdims=True)
        if mode != "ctx":
            s_l = _dot_nt(qs, kl)
            if mode == "window":
                s_l = jnp.where(in_window, s_l, -1e30)
            m = jnp.maximum(m, jnp.max(s_l, axis=-1, keepdims=True))
        if use_sink:
            sink = jnp.concatenate(
                [jnp.full((tq, 1), sink_ref[hp * 2 * GROUP + s * GROUP + g], F32) for g in range(GROUP)], axis=0)
            m = jnp.maximum(m, sink)
        p_c = jnp.exp(s_c - m)
        den = jnp.sum(p_c, axis=-1, keepdims=True)
        acc = _dot(p_c.astype(BF16), vc)
        if mode != "ctx":
            p_l = jnp.exp(s_l - m)
            den = den + jnp.sum(p_l, axis=-1, keepdims=True)
            acc = acc + _dot(p_l.astype(BF16), vl)
        if use_sink:
            den = den + jnp.exp(sink - m)
        outs.append(acc / den)
    for c in range(GROUP):
        o_ref[:, c * LANES:(c + 1) * LANES] = jnp.where(
            low, outs[0][c * tq:(c + 1) * tq], outs[1][c * tq:(c + 1) * tq]).astype(BF16)


def _attention(cfg, q, k, v, sink, mode, use_sink):
    B, L, LC, D = cfg.B, cfg.L, cfg.LC, cfg.D
    pairs = D // (2 * GROUP * HEAD_DIM)
    qw = GROUP * LANES
    if mode == "ctx":
        tq = min(cfg.TQ, LC)
        nq, row0 = LC // tq, cfg.ML // tq
        rows_out = cfg.MC
    else:
        tq = cfg.TQ
        nq, row0 = L // tq, 0
        rows_out = cfg.ML
    q_map = lambda b, hp, qi, *_: (row0 + b * nq + qi, hp)
    o_map = lambda b, hp, qi, *_: (b * nq + qi, hp)
    ctx_map = lambda b, hp, qi, *_: (cfg.ML // LC + b, hp)
    lat_map = lambda b, hp, qi, *_: (b, hp)
    in_specs = [pl.BlockSpec((tq, qw), q_map)]
    args = [q]
    if mode != "ctx":
        in_specs += [pl.BlockSpec((L, LANES), lat_map), pl.BlockSpec((L, LANES), lat_map)]
        args += [k, v]
    in_specs += [pl.BlockSpec((LC, LANES), ctx_map), pl.BlockSpec((LC, LANES), ctx_map)]
    args += [k, v]
    return pl.pallas_call(
        functools.partial(_attn_kernel, mode=mode, tq=tq, seq=L, window=cfg.WINDOW, use_sink=use_sink),
        grid_spec=pltpu.PrefetchScalarGridSpec(
            num_scalar_prefetch=1, grid=(B, pairs, nq), in_specs=in_specs,
            out_specs=pl.BlockSpec((tq, qw), o_map)),
        out_shape=jax.ShapeDtypeStruct((rows_out, D), BF16),
        compiler_params=_cparams(("parallel", "parallel", "parallel")),
        name="attn_" + mode,
    )(sink, *args)


def _proj_res_kernel(a_ref, w_ref, b_ref, gt_ref, x_ref, o_ref):
    y = _dot(a_ref[...], w_ref[...]) + b_ref[...]
    o_ref[...] = x_ref[...] + gt_ref[0] * y


def _proj_residual(cfg, a, w, bias, mod3, gate_k, x, n_rows):
    D, tm = cfg.D, cfg.TM
    K = a.shape[1]
    return pl.pallas_call(
        _proj_res_kernel,
        grid=(n_rows // tm,),
        in_specs=[pl.BlockSpec((tm, K), lambda i: (i, 0)),
                  pl.BlockSpec((K, D), lambda i: (0, 0)),
                  pl.BlockSpec((1, D), lambda i: (0, 0)),
                  _mod_spec(cfg, tm, gate_k),
                  pl.BlockSpec((tm, D), lambda i: (i, 0))],
        out_specs=pl.BlockSpec((tm, D), lambda i: (i, 0)),
        out_shape=jax.ShapeDtypeStruct((n_rows, D), F32),
        compiler_params=_cparams(("parallel",)),
        name="proj_residual",
    )(a, w, bias, mod3, x)


def _ffn_kernel(x_ref, sh_ref, sc_ref, gt_ref, g_ref, wg_ref, wu_ref, wd_ref, o_ref, h_scr, acc_scr):
    j = pl.program_id(1)

    @pl.when(j == 0)
    def _():
        h_scr[...] = _modnorm(x_ref[...], g_ref[...], sh_ref[0], sc_ref[0]).astype(BF16)
        acc_scr[...] = jnp.zeros_like(acc_scr)

    h = h_scr[...]
    hid = _silu(_dot(h, wg_ref[...])) * _dot(h, wu_ref[...])
    acc_scr[...] += _dot(hid.astype(BF16), wd_ref[...])

    @pl.when(j == pl.num_programs(1) - 1)
    def _():
        o_ref[...] = x_ref[...] + gt_ref[0] * acc_scr[...]


def _dense_ffn(cfg, x, mod3, g, wg, wu, wd, n_rows):
    D, tm, tf = cfg.D, cfg.TM_FFN, cfg.TF
    ff = wg.shape[1]
    return pl.pallas_call(
        _ffn_kernel,
        grid=(n_rows // tm, ff // tf),
        in_specs=[pl.BlockSpec((tm, D), lambda i, j: (i, 0)),
                  _mod_spec(cfg, tm, 3), _mod_spec(cfg, tm, 4), _mod_spec(cfg, tm, 5),
                  pl.BlockSpec((1, D), lambda i, j: (0, 0)),
                  pl.BlockSpec((D, tf), lambda i, j: (0, j)),
                  pl.BlockSpec((D, tf), lambda i, j: (0, j)),
                  pl.BlockSpec((tf, D), lambda i, j: (j, 0))],
        out_specs=pl.BlockSpec((tm, D), lambda i, j: (i, 0)),
        out_shape=jax.ShapeDtypeStruct((n_rows, D), F32),
        scratch_shapes=[pltpu.VMEM((tm, D), BF16), pltpu.VMEM((tm, D), F32)],
        compiler_params=_cparams(("parallel", "arbitrary")),
        name="dense_ffn",
    )(x, mod3, mod3, mod3, g, wg, wu, wd)


def _router_kernel(x_ref, sh_ref, sc_ref, g_ref, r_ref, h_ref, idx_ref, wt_ref):
    h = _modnorm(x_ref[...], g_ref[...], sh_ref[0], sc_ref[0])
    h_ref[...] = h.astype(BF16)
    logits = _dot3(h, r_ref[...])
    lane = lax.broadcasted_iota(I32, logits.shape, 1)
    lg = jnp.where(lane < N_EXPERTS, logits, -1e30)
    m1 = jnp.max(lg, axis=-1, keepdims=True)
    i1 = jnp.min(jnp.where(lg == m1, lane, LANES), axis=-1, keepdims=True)
    lg2 = jnp.where(lane == i1, -1e30, lg)
    m2 = jnp.max(lg2, axis=-1, keepdims=True)
    i2 = jnp.min(jnp.where(lg2 == m2, lane, LANES), axis=-1, keepdims=True)
    e = jnp.exp(m2 - m1)
    w1 = 1.0 / (1.0 + e)
    w2 = e / (1.0 + e)
    idx_ref[...] = jnp.where(lane == 0, i1, jnp.where(lane == 1, i2, 0))
    wt_ref[...] = jnp.where(lane == 0, w1, jnp.where(lane == 1, w2, 0.0))


def _router(cfg, x, mod3, g, router_pad, n_rows):
    D, tm = cfg.D, cfg.TM
    return pl.pallas_call(
        _router_kernel,
        grid=(n_rows // tm,),
        in_specs=[pl.BlockSpec((tm, D), lambda i: (i, 0)),
                  _mod_spec(cfg, tm, 3), _mod_spec(cfg, tm, 4),
                  pl.BlockSpec((1, D), lambda i: (0, 0)),
                  pl.BlockSpec((D, LANES), lambda i: (0, 0))],
        out_specs=[pl.BlockSpec((tm, D), lambda i: (i, 0)),
                   pl.BlockSpec((tm, LANES), lambda i: (i, 0)),
                   pl.BlockSpec((tm, LANES), lambda i: (i, 0))],
        out_shape=[jax.ShapeDtypeStruct((n_rows, D), BF16),
                   jax.ShapeDtypeStruct((n_rows, LANES), I32),
                   jax.ShapeDtypeStruct((n_rows, LANES), F32)],
        compiler_params=_cparams(("parallel",)),
        name="moe_router",
    )(x, mod3, mod3, g, router_pad)


def _moe_plan(cfg, idx, wt, n_rows):
    ts, tsg, tb = cfg.TS, cfg.TSG, cfg.TB
    A = 2 * n_rows
    S = A + N_EXPERTS * ts
    e = idx[:, :2].reshape(A)
    w = wt[:, :2].reshape(A)
    onehot = (e[:, None] == jnp.arange(N_EXPERTS, dtype=I32)[None, :]).astype(I32)
    csum = jnp.cumsum(onehot, axis=0)
    rank = jnp.sum((csum - 1) * onehot, axis=1)
    counts = csum[-1]
    padded = ((counts + ts - 1) // ts) * ts
    ends = jnp.cumsum(padded)
    slot = (ends - padded)[e] + rank
    tok = jnp.arange(A, dtype=I32) // 2
    src = jnp.full((S,), -1, I32).at[slot].set(tok)
    wslot = jnp.zeros((S,), F32).at[slot].set(w)
    tile_e = jnp.minimum(jnp.searchsorted(ends, jnp.arange(S // ts, dtype=I32) * ts, side="right"),
                         N_EXPERTS - 1).astype(I32)
    n_used = (ends[-1] // ts).astype(I32).reshape(1)
    src_t = src.reshape(S // tsg, tsg)
    lo = jnp.min(jnp.where(src_t >= 0, src_t, n_rows), axis=1)
    hi = jnp.max(src_t, axis=1)
    lo_blk = jnp.where(hi >= 0, lo // tb, 0).astype(I32)
    hi_blk = jnp.where(hi >= 0, hi // tb, -1).astype(I32)
    return src, wslot, tile_e, n_used, lo_blk, hi_blk


def _gather_kernel(lo_ref, hi_ref, src_ref, h_ref, o_ref, *, tb):
    i = pl.program_id(0)
    src = src_ref[...]
    ts = src.shape[0]
    lane = lax.broadcasted_iota(I32, (ts, tb), 1)

    def body(blk, acc):
        start = pl.multiple_of(blk * tb, tb)
        onehot = jnp.where(src - start == lane, 1.0, 0.0).astype(BF16)
        return acc + _dot(onehot, h_ref[pl.ds(start, tb), :])

    acc = lax.fori_loop(lo_ref[i], hi_ref[i] + 1, body, jnp.zeros(o_ref.shape, F32))
    o_ref[...] = acc.astype(BF16)


def _moe_gather(cfg, h, src, lo_blk, hi_blk):
    D, ts, tb = cfg.D, cfg.TSG, cfg.TB
    S, M = src.shape[0], h.shape[0]
    return pl.pallas_call(
        functools.partial(_gather_kernel, tb=tb),
        grid_spec=pltpu.PrefetchScalarGridSpec(
            num_scalar_prefetch=2, grid=(S // ts,),
            in_specs=[pl.BlockSpec((ts, 1), lambda i, *_: (i, 0)),
                      pl.BlockSpec((M, D), lambda i, *_: (0, 0), pipeline_mode=pl.Buffered(1))],
            out_specs=pl.BlockSpec((ts, D), lambda i, *_: (i, 0))),
        out_shape=jax.ShapeDtypeStruct((S, D), BF16),
        compiler_params=_cparams(("arbitrary",)),
        name="moe_gather",
    )(lo_blk, hi_blk, src.reshape(S, 1), h)


def _expert_up_kernel(te_ref, nu_ref, x_ref, wg_ref, wu_ref, o_ref):
    used = pl.program_id(0) < nu_ref[0]

    @pl.when(used)
    def _():
        x = x_ref[...]
        o_ref[...] = (_silu(_dot(x, wg_ref[0])) * _dot(x, wu_ref[0])).astype(BF16)

    @pl.when(jnp.logical_not(used))
    def _():
        o_ref[...] = jnp.zeros_like(o_ref)


def _expert_up(cfg, xs, wg, wu, tile_e, n_used):
    D, ts, tf = cfg.D, cfg.TS, cfg.TFE
    S, ff = xs.shape[0], wg.shape[2]
    return pl.pallas_call(
        _expert_up_kernel,
        grid_spec=pltpu.PrefetchScalarGridSpec(
            num_scalar_prefetch=2, grid=(S // ts, ff // tf),
            in_specs=[pl.BlockSpec((ts, D), lambda i, j, *_: (i, 0)),
                      pl.BlockSpec((1, D, tf), lambda i, j, te, nu: (te[i], 0, j)),
                      pl.BlockSpec((1, D, tf), lambda i, j, te, nu: (te[i], 0, j))],
            out_specs=pl.BlockSpec((ts, tf), lambda i, j, *_: (i, j))),
        out_shape=jax.ShapeDtypeStruct((S, ff), BF16),
        compiler_params=_cparams(("parallel", "arbitrary")),
        name="expert_up",
    )(tile_e, n_used, xs, wg, wu)


def _expert_down_kernel(te_ref, nu_ref, h_ref, wd_ref, ws_ref, hi_ref, lo_ref):
    used = pl.program_id(0) < nu_ref[0]

    @pl.when(used)
    def _():
        y = _dot(h_ref[...], wd_ref[0]) * ws_ref[...]
        hi, lo = _split(y)
        hi_ref[...] = hi
        lo_ref[...] = lo

    @pl.when(jnp.logical_not(used))
    def _():
        hi_ref[...] = jnp.zeros_like(hi_ref)
        lo_ref[...] = jnp.zeros_like(lo_ref)


def _expert_down(cfg, hs, wd, wslot, tile_e, n_used):
    D, ts = cfg.D, cfg.TS
    S, ff = hs.shape
    return pl.pallas_call(
        _expert_down_kernel,
        grid_spec=pltpu.PrefetchScalarGridSpec(
            num_scalar_prefetch=2, grid=(S // ts,),
            in_specs=[pl.BlockSpec((ts, ff), lambda i, *_: (i, 0)),
                      pl.BlockSpec((1, ff, D), lambda i, te, nu: (te[i], 0, 0)),
                      pl.BlockSpec((ts, 1), lambda i, *_: (i, 0))],
            out_specs=[pl.BlockSpec((ts, D), lambda i, *_: (i, 0)),
                       pl.BlockSpec((ts, D), lambda i, *_: (i, 0))]),
        out_shape=[jax.ShapeDtypeStruct((S, D), BF16), jax.ShapeDtypeStruct((S, D), BF16)],
        compiler_params=_cparams(("parallel",)),
        name="expert_down",
    )(tile_e, n_used, hs, wd, wslot.reshape(S, 1))


def _combine_kernel(lo_ref, hi_ref, src_ref, yh_ref, yl_ref, gt_ref, x_hbm, o_ref, sem, *, tb, seq, n_batch):
    c, i = pl.program_id(0), pl.program_id(1)
    tc = o_ref.shape[1]

    @pl.when(i == 0)
    def _():
        cp = pltpu.make_async_copy(x_hbm.at[:, pl.ds(pl.multiple_of(c * tc, tc), tc)], o_ref, sem)
        cp.start()
        cp.wait()

    src = src_ref[0]
    ts = src.shape[1]
    yh, yl = yh_ref[...], yl_ref[...]
    sub = lax.broadcasted_iota(I32, (tb, ts), 0)

    def body(blk, carry):
        start = pl.multiple_of(blk * tb, tb)
        onehot = jnp.where(src - start == sub, 1.0, 0.0).astype(BF16)
        contrib = _dot(onehot, yh) + _dot(onehot, yl)
        gate = gt_ref[pl.ds(jnp.minimum(start // seq, n_batch), 1), :]
        o_ref[pl.ds(start, tb), :] += gate * contrib
        return carry

    lax.fori_loop(lo_ref[i], hi_ref[i] + 1, body, 0)


def _moe_combine(cfg, x, y_hi, y_lo, src, gate_rows, lo_blk, hi_blk, n_rows):
    D, ts, tb, tc = cfg.D, cfg.TSG, cfg.TB, cfg.TC
    S = src.shape[0]
    return pl.pallas_call(
        functools.partial(_combine_kernel, tb=tb, seq=cfg.L, n_batch=cfg.B),
        grid_spec=pltpu.PrefetchScalarGridSpec(
            num_scalar_prefetch=2, grid=(D // tc, S // ts),
            in_specs=[pl.BlockSpec((1, 1, ts), lambda c, i, *_: (i, 0, 0)),
                      pl.BlockSpec((ts, tc), lambda c, i, *_: (i, c)),
                      pl.BlockSpec((ts, tc), lambda c, i, *_: (i, c)),
                      pl.BlockSpec((16, tc), lambda c, i, *_: (0, c)),
                      pl.BlockSpec(memory_space=pl.ANY)],
            out_specs=pl.BlockSpec((n_rows, tc), lambda c, i, *_: (0, c)),
            scratch_shapes=[pltpu.SemaphoreType.DMA(())]),
        out_shape=jax.ShapeDtypeStruct((n_rows, D), F32),
        compiler_params=_cparams(("arbitrary", "arbitrary")),
        name="moe_combine",
    )(lo_blk, hi_blk, src.reshape(S // ts, 1, ts), y_hi, y_lo, gate_rows, x)


def _moe_layer(cfg, x, mod3, g, router_pad, wg, wu, wd, n_rows):
    h, idx, wt = _router(cfg, x, mod3, g, router_pad, n_rows)
    src, wslot, tile_e, n_used, lo_blk, hi_blk = _moe_plan(cfg, idx, wt, n_rows)
    xs = _moe_gather(cfg, h, src, lo_blk, hi_blk)
    hs = _expert_up(cfg, xs, wg, wu, tile_e, n_used)
    y_hi, y_lo = _expert_down(cfg, hs, wd, wslot, tile_e, n_used)
    gate_rows = mod3.reshape(16, 6, cfg.D)[:, 5, :]
    return _moe_combine(cfg, x, y_hi, y_lo, src, gate_rows, lo_blk, hi_blk, n_rows)


def _dft_matrix(n_half):
    n = 2 * n_half
    nb = 64
    r = jnp.arange(n, dtype=I32)
    is_cos = (r <= n_half)[:, None]
    k = jnp.where(r <= n_half, r, r - n_half)[:, None]
    ang_a = ((k * (jnp.arange(n_half // nb, dtype=I32) * nb)[None, :]) % n).astype(F32) * (2.0 * np.pi / n)
    ang_b = ((k * jnp.arange(nb, dtype=I32)[None, :]) % n).astype(F32) * (2.0 * np.pi / n)
    ca = jnp.where(is_cos, jnp.cos(ang_a), jnp.sin(ang_a))[:, :, None]
    sa = jnp.where(is_cos, jnp.sin(ang_a), -jnp.cos(ang_a))[:, :, None]
    cb, sb = jnp.cos(ang_b)[:, None, :], jnp.sin(ang_b)[:, None, :]
    return (ca * cb - sa * sb).reshape(n, n_half)


def _hy_filter_kernel(z_ref, w1_ref, b1_ref, w2_ref, b2_ref, w3f_ref, w3b_ref, f_ref, df_ref, db_ref,
                      ft_ref, fb_ref, tp_ref, ga_ref, gb_ref, gc_ref, sum_scr, dif_scr, nrm_scr, nyq_scr, *, seq):
    i = pl.program_id(1)
    tm = ft_ref.shape[0]
    n = 2 * seq

    @pl.when(i == 0)
    def _():
        z = z_ref[...]
        a = jnp.sin(f_ref[0:1, :] * (_dot3(z, w1_ref[...]) + b1_ref[...]))
        a = jnp.sin(f_ref[1:2, :] * (_dot3(a, w2_ref[...]) + b2_ref[...]))
        t_unit = z[:, 0:1]
        row = lax.broadcasted_iota(I32, (seq, 1), 0)
        hf = _dot3(a, w3f_ref[...]) * jnp.exp(-t_unit * jnp.exp(df_ref[...]))
        hb = _dot3(a, w3b_ref[...]) * jnp.exp(-t_unit * jnp.exp(db_ref[...]))
        hb = jnp.where(row == 0, 0.0, hb)
        nrm_scr[...] = jnp.sum(jnp.abs(hf) + jnp.abs(hb), axis=0, keepdims=True)
        tot = hf + hb
        sum_scr[...] = tot.astype(BF16)
        dif_scr[...] = (hf - hb).astype(BF16)
        nyq_scr[...] = jnp.sum(jnp.where((row & 1) == 0, tot, -tot), axis=0, keepdims=True)

    p = _dot(ft_ref[...], sum_scr[...])
    q = _dot(fb_ref[...], dif_scr[...])
    inv = 1.0 / nrm_scr[...]
    first = (i * tm + lax.broadcasted_iota(I32, (tm, 1), 0)) == 0
    w_one, w_two = inv * (1.0 / n), inv * (2.0 / n)
    wt = jnp.where(first, w_one, w_two)
    tp_ref[...] = wt * p
    ga_ref[...] = jnp.where(first, 0.0, -(w_two * q))
    gb_ref[...] = jnp.where(first, 0.0, w_two * q)
    gc_ref[...] = jnp.where(first, w_one * nyq_scr[...], w_two * p)


def _pad_to(a, shape):
    return jnp.zeros(shape, a.dtype).at[tuple(slice(0, d) for d in a.shape)].set(a)


def _hy_filter(cfg, seq, z, w1, b1, w2, b2, w3, sin_freq, log_decay, fmat):
    D, tc = cfg.D, cfg.TC
    tm = min(cfg.TMF, seq)
    pw = LANES
    full = lambda shp: pl.BlockSpec(shp, lambda j, i: (0,) * len(shp))
    col = lambda rows, off: pl.BlockSpec((rows, tc), lambda j, i: (0, off + j))
    out_spec = pl.BlockSpec((tm, tc), lambda j, i: (i, j))
    w3p = _pad_to(w3, (pw, 2 * D))
    return pl.pallas_call(
        functools.partial(_hy_filter_kernel, seq=seq),
        grid=(D // tc, seq // tm),
        in_specs=[full((seq, pw)), full((pw, pw)), full((1, pw)), full((pw, pw)), full((1, pw)),
                  col(pw, 0), col(pw, D // tc), full((2, pw)), col(1, 0), col(1, D // tc),
                  pl.BlockSpec((tm, seq), lambda j, i: (i, 0)),
                  pl.BlockSpec((tm, seq), lambda j, i: (seq // tm + i, 0))],
        out_specs=[out_spec] * 4,
        out_shape=[jax.ShapeDtypeStruct((seq, D), F32)] * 4,
        scratch_shapes=[pltpu.VMEM((seq, tc), BF16), pltpu.VMEM((seq, tc), BF16),
                        pltpu.VMEM((1, tc), F32), pltpu.VMEM((1, tc), F32)],
        compiler_params=_cparams(("arbitrary", "arbitrary")),
        name="hyena_filter",
    )(_pad_to(z, (seq, pw)), _pad_to(w1, (pw, pw)), _pad_to(b1.reshape(1, -1), (1, pw)),
      _pad_to(w2, (pw, pw)), _pad_to(b2.reshape(1, -1), (1, pw)), w3p, w3p, _pad_to(sin_freq, (2, pw)),
      log_decay.reshape(1, 2 * D), log_decay.reshape(1, 2 * D), fmat, fmat)


def _hy_in_kernel(x_ref, sh_ref, sc_ref, g_ref, w0_ref, w1_ref, w2_ref, b_ref, cw_ref, cb_ref,
                  x0_ref, u_ref, ub_ref, h_scr):
    j = pl.program_id(1)
    seq = x_ref.shape[0]

    @pl.when(j == 0)
    def _():
        h_scr[...] = _modnorm(x_ref[...], g_ref[...], sh_ref[0], sc_ref[0]).astype(BF16)

    h = h_scr[...]
    row = lax.broadcasted_iota(I32, (seq, 1), 0)

    def branch(w_ref, t):
        p = _dot(h, w_ref[...]) + b_ref[t]
        prev = jnp.where(row == 0, 0.0, pltpu.roll(p, 1, 0))
        nxt = jnp.where(row == seq - 1, 0.0, pltpu.roll(p, seq - 1, 0))
        cw = cw_ref[t]
        return cw[0:1] * prev + cw[1:2] * p + cw[2:3] * nxt + cb_ref[t]

    x0_ref[...] = branch(w0_ref, 0)
    u = branch(w1_ref, 1) * branch(w2_ref, 2)
    u_ref[...] = u
    ub_ref[...] = u.astype(BF16)


def _hy_in(cfg, seq, row_blk0, x, mod3, g, w, b3, cw3, cb3):
    D, tc = cfg.D, cfg.TC
    nj = D // tc
    rows = cfg.B * seq
    mrow = (lambda b: b) if row_blk0 == 0 else (lambda b: cfg.B)
    mod = lambda k: pl.BlockSpec((1, 1, D), lambda b, j: (mrow(b) * 6 + k, 0, 0))
    wspec = lambda t: pl.BlockSpec((D, tc), lambda b, j: (0, t * nj + j))
    out = pl.BlockSpec((seq, tc), lambda b, j: (b, j))
    return pl.pallas_call(
        _hy_in_kernel,
        grid=(cfg.B, nj),
        in_specs=[pl.BlockSpec((seq, D), lambda b, j: (row_blk0 + b, 0)), mod(0), mod(1),
                  pl.BlockSpec((1, D), lambda b, j: (0, 0)),
                  wspec(0), wspec(1), wspec(2),
                  pl.BlockSpec((3, 1, tc), lambda b, j: (0, 0, j)),
                  pl.BlockSpec((3, 3, tc), lambda b, j: (0, 0, j)),
                  pl.BlockSpec((3, 1, tc), lambda b, j: (0, 0, j))],
        out_specs=[out, out, out],
        out_shape=[jax.ShapeDtypeStruct((rows, D), F32), jax.ShapeDtypeStruct((rows, D), F32),
                   jax.ShapeDtypeStruct((rows, D), BF16)],
        scratch_shapes=[pltpu.VMEM((seq, D), BF16)],
        compiler_params=_cparams(("parallel", "arbitrary")),
        name="hyena_in",
    )(x, mod3, mod3, g, w, w, w, b3, cw3, cb3)


def _hy_fwd_kernel(ft_ref, fb_ref, u_ref, tp_ref, ga_ref, gb_ref, gc_ref, zt_ref, zb_ref):
    u = u_ref[...]
    a = _dot(ft_ref[...], u)
    b = _dot(fb_ref[...], u)
    zt_ref[0] = (a * tp_ref[...] + b * ga_ref[...]).astype(BF16)
    zb_ref[0] = (a * gb_ref[...] + b * gc_ref[...]).astype(BF16)


def _hy_fwd(cfg, seq, ub, fmat, filt):
    D, tc = cfg.D, cfg.TC
    tm = min(cfg.TMF, seq)
    nt = seq // tm
    g_spec = pl.BlockSpec((tm, tc), lambda b, j, i: (i, j))
    return pl.pallas_call(
        _hy_fwd_kernel,
        grid=(cfg.B, D // tc, nt),
        in_specs=[pl.BlockSpec((tm, seq), lambda b, j, i: (i, 0)),
                  pl.BlockSpec((tm, seq), lambda b, j, i: (nt + i, 0)),
                  pl.BlockSpec((seq, tc), lambda b, j, i: (b, j)),
                  g_spec, g_spec, g_spec, g_spec],
        out_specs=[pl.BlockSpec((1, tm, tc), lambda b, j, i: (b, i, j))] * 2,
        out_shape=[jax.ShapeDtypeStruct((cfg.B, seq, D), BF16)] * 2,
        compiler_params=_cparams(("parallel", "parallel", "arbitrary")),
        name="hyena_fwd_dft",
    )(fmat, fmat, ub, *filt)


def _hy_inv_kernel(fi_ref, zt_ref, zb_ref, u_ref, x0_ref, skip_ref, y_ref, *, seq):
    fi = fi_ref[...]
    conv = _dot(fi[:, :seq], zt_ref[0]) + _dot(fi[:, seq:], zb_ref[0])
    v = conv + skip_ref[...] * u_ref[...]
    y_ref[...] = (x0_ref[...] * v).astype(BF16)


def _hy_inv(cfg, seq, zt, zb, finv, u, x0, skip):
    D, tc = cfg.D, cfg.TC
    tm = min(cfg.TMF, seq)
    io = pl.BlockSpec((tm, tc), lambda b, j, i: (b * (seq // tm) + i, j))
    z_spec = pl.BlockSpec((1, seq, tc), lambda b, j, i: (b, 0, j))
    return pl.pallas_call(
        functools.partial(_hy_inv_kernel, seq=seq),
        grid=(cfg.B, D // tc, seq // tm),
        in_specs=[pl.BlockSpec((tm, 2 * seq), lambda b, j, i: (i, 0)), z_spec, z_spec, io, io,
                  pl.BlockSpec((1, tc), lambda b, j, i: (0, j))],
        out_specs=io,
        out_shape=jax.ShapeDtypeStruct((cfg.B * seq, D), BF16),
        compiler_params=_cparams(("parallel", "parallel", "arbitrary")),
        name="hyena_inv_dft",
    )(finv, zt, zb, u, x0, skip)


def _hy_positions(seq):
    t = jnp.arange(seq, dtype=F32)
    t_unit = t / max(seq - 1, 1)
    phase = 2.0 * np.pi * t / seq
    bands = jnp.linspace(1e-4, HY_BANDS - 1, HY_BANDS, dtype=F32)
    ang = phase[:, None] * bands[None, :]
    return jnp.concatenate([t_unit[:, None], jnp.cos(ang), -jnp.sin(ang)], axis=-1)


def _hyena_seq(cfg, seq, row_blk0, x, mod3, g, w_in, b3, cw3, cb3, filt_params, skip):
    fmat = _dft_matrix(seq)
    fb16 = fmat.astype(BF16)
    filt = _hy_filter(cfg, seq, _hy_positions(seq), *filt_params, fb16)
    x0, u, ub = _hy_in(cfg, seq, row_blk0, x, mod3, g, w_in, b3, cw3, cb3)
    zt, zb = _hy_fwd(cfg, seq, ub, fb16, filt)
    return _hy_inv(cfg, seq, zt, zb, fmat.T.astype(BF16), u, x0, skip)


def _final_norm_kernel(x_ref, g_ref, o_ref):
    x = x_ref[...]
    o_ref[...] = (x * lax.rsqrt(jnp.mean(x * x, axis=-1, keepdims=True) + NORM_EPS)) * g_ref[...]


def _final_norm(cfg, x, g):
    D, tm = cfg.D, cfg.TM
    return pl.pallas_call(
        _final_norm_kernel,
        grid=(cfg.ML // tm,),
        in_specs=[pl.BlockSpec((tm, D), lambda i: (i, 0)), pl.BlockSpec((1, D), lambda i: (0, 0))],
        out_specs=pl.BlockSpec((tm, D), lambda i: (i, 0)),
        out_shape=jax.ShapeDtypeStruct((cfg.ML, D), F32),
        compiler_params=_cparams(("parallel",)),
        name="final_norm",
    )(x, g)


def _rope_tables(cfg):
    L, tm = cfg.L, cfg.TM
    t = jnp.arange(L, dtype=I32)
    row, col = (t // cfg.GRID_W).astype(F32), (t % cfg.GRID_W).astype(F32)
    inv = ROPE_BASE ** (-jnp.arange(ROPE_FREQS, dtype=F32) / ROPE_FREQS)
    a0, a1 = row[:, None] * inv, col[:, None] * inv
    cos_h = jnp.concatenate([jnp.cos(a0), jnp.cos(a0), jnp.cos(a1), jnp.cos(a1)], axis=1)
    sin_h = jnp.concatenate([-jnp.sin(a0), jnp.sin(a0), -jnp.sin(a1), jnp.sin(a1)], axis=1)
    cos_t = jnp.concatenate([jnp.tile(cos_h, (1, 2)), jnp.ones((tm, LANES), F32)], axis=0)
    sin_t = jnp.concatenate([jnp.tile(sin_h, (1, 2)), jnp.zeros((tm, LANES), F32)], axis=0)
    return cos_t, sin_t


def _head_perm(D):
    heads = []
    for hp in range(D // (2 * GROUP * HEAD_DIM)):
        for c in range(GROUP):
            heads += [2 * GROUP * hp + c, 2 * GROUP * hp + GROUP + c]
    return np.concatenate([np.arange(h * HEAD_DIM, (h + 1) * HEAD_DIM) for h in heads])


def _attn_layer(cfg, mode, last, x, mod3, g, w_in, w_out, q_g, k_g, sink, tables):
    D = cfg.D
    perm = _head_perm(D)
    w = jnp.concatenate([w_in[:, :D][:, perm], w_in[:, D:]], axis=1).astype(BF16)
    wo = w_out[perm, :].astype(BF16)
    qg = jnp.tile(q_g * (HEAD_DIM ** -0.5), 2).reshape(1, LANES)
    kg = jnp.tile(k_g, 2).reshape(1, LANES)
    blk = np.kron(np.eye(2, dtype=np.float32), np.ones((HEAD_DIM, HEAD_DIM), np.float32))
    cos_t, sin_t = tables
    q, k, v = _qkv_proj(cfg, x, mod3, g, w, qg, kg, cos_t, sin_t, jnp.asarray(blk, BF16))
    use_sink = sink is not None
    sink = sink.astype(F32) if use_sink else jnp.zeros((D // HEAD_DIM,), F32)
    o = _attention(cfg, q, k, v, sink, mode, use_sink)
    n_rows = cfg.ML if last else cfg.MT
    if not last:
        oc = _attention(cfg, q, k, v, sink, "ctx", use_sink)
        o = jnp.concatenate([o, oc], axis=0)
    return _proj_residual(cfg, o, wo, jnp.zeros((1, D), F32), mod3, 2, x, n_rows)


def _hyena_layer(cfg, last, x, mod3, g, w_in, b_in, conv_w, conv_b, filt_params, skip, w_out, b_out):
    D = cfg.D
    w = w_in.astype(BF16)
    b3, cb3 = b_in.reshape(3, 1, D), conv_b.reshape(3, 1, D)
    cw3 = conv_w.reshape(3, 3, D).transpose(1, 0, 2)
    skip = skip.reshape(1, D)
    y = _hyena_seq(cfg, cfg.L, 0, x, mod3, g, w, b3, cw3, cb3, filt_params, skip)
    n_rows = cfg.ML
    if not last:
        yc = _hyena_seq(cfg, cfg.LC, cfg.ML // cfg.LC, x, mod3, g, w, b3, cw3, cb3, filt_params, skip)
        y = jnp.concatenate([y, yc], axis=0)
        n_rows = cfg.MT
    return _proj_residual(cfg, y, w_out.astype(BF16), b_out.reshape(1, D), mod3, 2, x, n_rows)


def _forward(cfg, x, c, ctx, c_ctx, mod_w, mod_b, norm_g,
             gattn_w_in, gattn_w_out, gattn_q_norm, gattn_k_norm,
             wattn_w_in, wattn_w_out, wattn_q_norm, wattn_k_norm, wattn_sink,
             hy_w_in, hy_b_in, hy_conv_w, hy_conv_b, hy_ffn_w1, hy_ffn_b1, hy_ffn_w2, hy_ffn_b2,
             hy_ffn_w3, hy_sin_freq, hy_log_decay, hy_skip, hy_w_out, hy_b_out,
             ffn_w_gate, ffn_w_up, ffn_w_down,
             moe_router, moe_w_gate, moe_w_up, moe_w_down, final_norm_g):
    D, depth = cfg.D, cfg.DEPTH
    xs = jnp.concatenate([x.reshape(cfg.ML, D), ctx.reshape(cfg.MC, D)], axis=0)
    mod = _mod_vectors(cfg, c, c_ctx, mod_w, mod_b)
    tables = _rope_tables(cfg)
    for i in range(depth):
        last = i == depth - 1
        mod3 = mod[i]
        kind, j = i % 3, i // 3
        g_a, g_f = norm_g[i, 0].reshape(1, D), norm_g[i, 1].reshape(1, D)
        if kind == 0:
            xs = _attn_layer(cfg, "global", last, xs, mod3, g_a, gattn_w_in[j], gattn_w_out[j],
                             gattn_q_norm[j], gattn_k_norm[j], None, tables)
        elif kind == 1:
            xs = _attn_layer(cfg, "window", last, xs, mod3, g_a, wattn_w_in[j], wattn_w_out[j],
                             wattn_q_norm[j], wattn_k_norm[j], wattn_sink[j], tables)
        else:
            filt = (hy_ffn_w1[j], hy_ffn_b1[j], hy_ffn_w2[j], hy_ffn_b2[j], hy_ffn_w3[j],
                    hy_sin_freq[j], hy_log_decay[j])
            xs = _hyena_layer(cfg, last, xs, mod3, g_a, hy_w_in[j], hy_b_in[j], hy_conv_w[j], hy_conv_b[j],
                              filt, hy_skip[j], hy_w_out[j], hy_b_out[j])
        n_rows = cfg.ML if last else cfg.MT
        k_ff = i // 2
        if i % 2 == 0:
            xs = _dense_ffn(cfg, xs, mod3, g_f, ffn_w_gate[k_ff].astype(BF16), ffn_w_up[k_ff].astype(BF16),
                            ffn_w_down[k_ff].astype(BF16), n_rows)
        else:
            router_pad = jnp.zeros((D, LANES), F32).at[:, :N_EXPERTS].set(moe_router[k_ff])
            xs = _moe_layer(cfg, xs, mod3, g_f, router_pad, moe_w_gate[k_ff].astype(BF16),
                            moe_w_up[k_ff].astype(BF16), moe_w_down[k_ff].astype(BF16), n_rows)
    return _final_norm(cfg, xs, final_norm_g.reshape(1, D)).reshape(cfg.B, cfg.L, D)


def kernel(x, c, ctx, c_ctx, mod_w, mod_b, norm_g, gattn_w_in, gattn_w_out, gattn_q_norm, gattn_k_norm, wattn_w_in, wattn_w_out, wattn_q_norm, wattn_k_norm, wattn_sink, hy_w_in, hy_b_in, hy_conv_w, hy_conv_b, hy_ffn_w1, hy_ffn_b1, hy_ffn_w2, hy_ffn_b2, hy_ffn_w3, hy_sin_freq, hy_log_decay, hy_skip, hy_w_out, hy_b_out, ffn_w_gate, ffn_w_up, ffn_w_down, moe_router, moe_w_gate, moe_w_up, moe_w_down, final_norm_g):
    return _forward(Cfg(), x, c, ctx, c_ctx, mod_w, mod_b, norm_g, gattn_w_in, gattn_w_out, gattn_q_norm,
                    gattn_k_norm, wattn_w_in, wattn_w_out, wattn_q_norm, wattn_k_norm, wattn_sink, hy_w_in,
                    hy_b_in, hy_conv_w, hy_conv_b, hy_ffn_w1, hy_ffn_b1, hy_ffn_w2, hy_ffn_b2, hy_ffn_w3,
                    hy_sin_freq, hy_log_decay, hy_skip, hy_w_out, hy_b_out, ffn_w_gate, ffn_w_up, ffn_w_down,
                    moe_router, moe_w_gate, moe_w_up, moe_w_down, final_norm_g)
```

```python
import functools
import math
from typing import NamedTuple

import numpy as np
import jax
import jax.numpy as jnp
from jax import lax
from jax.experimental import pallas as pl
from jax.experimental.pallas import tpu as pltpu

F32, BF16, I32 = jnp.float32, jnp.bfloat16, jnp.int32

HEAD_DIM = 64
GROUP = 4
ROPE_BASE = 10000.0
ROPE_FREQS = HEAD_DIM // 4
HY_BANDS = 16
NORM_EPS = 1e-6
N_EXPERTS = 8
LANES = 128
VMEM_LIMIT = 56 * 1024 * 1024


class Cfg(NamedTuple):
    B: int = 8
    L: int = 2048
    LC: int = 256
    D: int = 1024
    GRID_W: int = 64
    WINDOW: int = 128
    DEPTH: int = 4
    TM: int = 512
    TM_FFN: int = 1024
    TF: int = 256
    TQ: int = 128
    TS: int = 512
    TD: int = 1024
    TMC: int = 512
    TFE: int = 512
    TND: int = 512
    TC: int = 256
    TMF: int = 512

    @property
    def ML(self):
        return self.B * self.L

    @property
    def MC(self):
        return self.B * self.LC

    @property
    def MT(self):
        return self.ML + self.MC


def _cparams(sem):
    return pltpu.CompilerParams(dimension_semantics=sem, vmem_limit_bytes=VMEM_LIMIT)


def _dot(a, b):
    return jnp.dot(a, b, preferred_element_type=F32)


def _dot_nt(a, b):
    return lax.dot_general(a, b, (((1,), (1,)), ((), ())), preferred_element_type=F32)


def _split(x):
    hi = x.astype(BF16)
    return hi, (x - hi.astype(F32)).astype(BF16)


def _dot3(a, b):
    ah, al = _split(a)
    bh, bl = _split(b)
    return _dot(ah, bh) + (_dot(al, bh) + _dot(ah, bl))


def _silu(a):
    return a * jax.nn.sigmoid(a)


def _modnorm(x, g, shift, scale):
    y = x * lax.rsqrt(jnp.mean(x * x, axis=-1, keepdims=True) + NORM_EPS)
    return (y * g) * (1.0 + scale) + shift


def _mod_row(cfg, i, tm):
    return jnp.minimum((i * tm) // cfg.L, cfg.B)


def _mod_spec(cfg, tm, k):
    D = cfg.D
    return pl.BlockSpec((1, 1, D), lambda i, *_: (_mod_row(cfg, i, tm) * 6 + k, 0, 0))


def _mod_kernel(c_ref, w_ref, b_ref, o_ref):
    o_ref[0] = _dot3(_silu(c_ref[...]), w_ref[0]) + b_ref[0]


def _mod_vectors(cfg, c, c_ctx, mod_w, mod_b):
    D, depth = cfg.D, cfg.DEPTH
    tn = 3 * D // 2
    rows = jnp.zeros((16, D), F32).at[:cfg.B].set(c).at[cfg.B].set(c_ctx)
    out = pl.pallas_call(
        _mod_kernel,
        grid=(depth, 6 * D // tn),
        in_specs=[pl.BlockSpec((16, D), lambda l, j: (0, 0)),
                  pl.BlockSpec((1, D, tn), lambda l, j: (l, 0, j)),
                  pl.BlockSpec((1, 1, tn), lambda l, j: (l, 0, j))],
        out_specs=pl.BlockSpec((1, 16, tn), lambda l, j: (l, 0, j)),
        out_shape=jax.ShapeDtypeStruct((depth, 16, 6 * D), F32),
        compiler_params=_cparams(("arbitrary", "arbitrary")),
        name="mod_vectors",
    )(rows, mod_w, mod_b.reshape(depth, 1, 6 * D))
    return out.reshape(depth, 16 * 6, 1, D)


def _qkv_kernel(x_ref, sh_ref, sc_ref, g_ref, w_ref, qg_ref, kg_ref, cos_ref, sin_ref, bd_ref,
                q_ref, k_ref, v_ref, *, attn_dim, kv_dim):
    h = _modnorm(x_ref[...], g_ref[...], sh_ref[0], sc_ref[0])
    p = _dot(h.astype(BF16), w_ref[...])
    tm = p.shape[0]
    upper = (lax.broadcasted_iota(I32, (tm, LANES), 1) & ROPE_FREQS) != 0
    cosv, sinv, bd = cos_ref[...], sin_ref[...], bd_ref[...]

    def norm_rope(xc, gain):
        hi, lo = _split(xc * xc)
        ss = _dot(hi, bd) + _dot(lo, bd)
        y = (xc * lax.rsqrt(ss * (1.0 / HEAD_DIM) + NORM_EPS)) * gain
        partner = jnp.where(upper, pltpu.roll(y, ROPE_FREQS, 1), pltpu.roll(y, LANES - ROPE_FREQS, 1))
        return y * cosv + partner * sinv

    for c in range(attn_dim // LANES):
        sl = slice(c * LANES, (c + 1) * LANES)
        q_ref[:, sl] = norm_rope(p[:, sl], qg_ref[...]).astype(BF16)
    for c in range(kv_dim // LANES):
        sl = slice(c * LANES, (c + 1) * LANES)
        k_ref[:, sl] = norm_rope(p[:, attn_dim + c * LANES: attn_dim + (c + 1) * LANES], kg_ref[...]).astype(BF16)
    v_ref[...] = p[:, attn_dim + kv_dim:].astype(BF16)


def _qkv_proj(cfg, x, mod3, g, w, qg, kg, cos_t, sin_t, bd):
    D, tm = cfg.D, cfg.TM
    kv = D // GROUP
    n_lat, per_seq = cfg.ML // tm, cfg.L // tm
    rope_map = lambda i: (jnp.where(i < n_lat, i % per_seq, per_seq), 0)
    full = lambda shp: pl.BlockSpec(shp, lambda i: (0,) * len(shp))
    return pl.pallas_call(
        functools.partial(_qkv_kernel, attn_dim=D, kv_dim=kv),
        grid=(cfg.MT // tm,),
        in_specs=[pl.BlockSpec((tm, D), lambda i: (i, 0)),
                  _mod_spec(cfg, tm, 0), _mod_spec(cfg, tm, 1),
                  full((1, D)), full((D, D + 2 * kv)), full((1, LANES)), full((1, LANES)),
                  pl.BlockSpec((tm, LANES), rope_map), pl.BlockSpec((tm, LANES), rope_map),
                  full((LANES, LANES))],
        out_specs=[pl.BlockSpec((tm, D), lambda i: (i, 0)),
                   pl.BlockSpec((tm, kv), lambda i: (i, 0)),
                   pl.BlockSpec((tm, kv), lambda i: (i, 0))],
        out_shape=[jax.ShapeDtypeStruct((cfg.MT, D), BF16),
                   jax.ShapeDtypeStruct((cfg.MT, kv), BF16),
                   jax.ShapeDtypeStruct((cfg.MT, kv), BF16)],
        compiler_params=_cparams(("parallel",)),
        name="qkv_proj",
    )(x, mod3, mod3, g, w, qg, kg, cos_t, sin_t, bd)


def _attn_tile(sink_ref, q_ref, kl_ref, vl_ref, kc_ref, vc_ref, o_ref, *, mode, tq, seq, window, use_sink):
    hp, qi = pl.program_id(1), pl.program_id(2)
    rows = GROUP * tq
    low = lax.broadcasted_iota(I32, (tq, LANES), 1) < HEAD_DIM
    kc, vc = kc_ref[...], vc_ref[...]
    if mode == "global":
        kl, vl = kl_ref[...], vl_ref[...]
    elif mode == "window":
        span = tq + 2 * window
        start = pl.multiple_of(jnp.clip(qi * tq - window, 0, seq - span), math.gcd(tq, window))
        kl, vl = kl_ref[pl.ds(start, span), :], vl_ref[pl.ds(start, span), :]
        qpos = qi * tq + (lax.broadcasted_iota(I32, (rows, span), 0) & (tq - 1))
        kpos = start + lax.broadcasted_iota(I32, (rows, span), 1)
        in_window = jnp.abs(qpos - kpos) <= window
    outs = []
    for s in range(2):
        keep = low if s == 0 else jnp.logical_not(low)
        qs = jnp.concatenate(
            [jnp.where(keep, q_ref[:, c * LANES:(c + 1) * LANES], jnp.zeros((), BF16)) for c in range(GROUP)],
            axis=0)
        s_c = _dot_nt(qs, kc)
        m = jnp.max(s_c, axis=-1, keepdims=True)
        if mode != "ctx":
            s_l = _dot_nt(qs, kl)
            if mode == "window":
                s_l = jnp.where(in_window, s_l, -1e30)
            m = jnp.maximum(m, jnp.max(s_l, axis=-1, keepdims=True))
        if use_sink:
            sink = jnp.concatenate(
                [jnp.full((tq, 1), sink_ref[hp * 2 * GROUP + s * GROUP + g], F32) for g in range(GROUP)], axis=0)
            m = jnp.maximum(m, sink)
        p_c = jnp.exp(s_c - m)
        den = jnp.sum(p_c, axis=-1, keepdims=True)
        acc = _dot(p_c.astype(BF16), vc)
        if mode != "ctx":
            p_l = jnp.exp(s_l - m)
            den = den + jnp.sum(p_l, axis=-1, keepdims=True)
            acc = acc + _dot(p_l.astype(BF16), vl)
        if use_sink:
            den = den + jnp.exp(sink - m)
        outs.append(acc / den)
    for c in range(GROUP):
        o_ref[:, c * LANES:(c + 1) * LANES] = jnp.where(
            low, outs[0][c * tq:(c + 1) * tq], outs[1][c * tq:(c + 1) * tq]).astype(BF16)


def _attn_kernel(*refs, mode, n_lat, with_ctx, **kw):
    if not with_ctx:
        _attn_tile(*refs, mode=mode, **kw)
        return
    qi = pl.program_id(2)
    pl.when(qi < n_lat)(lambda: _attn_tile(*refs, mode=mode, **kw))
    pl.when(qi >= n_lat)(lambda: _attn_tile(*refs, mode="ctx", **kw))


def _attention(cfg, q, k, v, sink, mode, use_sink, with_ctx):
    B, L, LC, D, tq = cfg.B, cfg.L, cfg.LC, cfg.D, cfg.TQ
    pairs = D // (2 * GROUP * HEAD_DIM)
    qw = GROUP * LANES
    n_lat, n_ctx = L // tq, (LC // tq if with_ctx else 0)
    q_map = lambda b, hp, qi, *_: (jnp.where(qi < n_lat, b * n_lat + qi, cfg.ML // tq + b * n_ctx + qi - n_lat), hp)
    ctx_map = lambda b, hp, qi, *_: (cfg.ML // LC + b, hp)
    lat_map = lambda b, hp, qi, *_: (b, hp)
    return pl.pallas_call(
        functools.partial(_attn_kernel, mode=mode, n_lat=n_lat, with_ctx=with_ctx, tq=tq, seq=L,
                          window=cfg.WINDOW, use_sink=use_sink),
        grid_spec=pltpu.PrefetchScalarGridSpec(
            num_scalar_prefetch=1, grid=(B, pairs, n_lat + n_ctx),
            in_specs=[pl.BlockSpec((tq, qw), q_map),
                      pl.BlockSpec((L, LANES), lat_map), pl.BlockSpec((L, LANES), lat_map),
                      pl.BlockSpec((LC, LANES), ctx_map), pl.BlockSpec((LC, LANES), ctx_map)],
            out_specs=pl.BlockSpec((tq, qw), q_map)),
        out_shape=jax.ShapeDtypeStruct((cfg.MT if with_ctx else cfg.ML, D), BF16),
        compiler_params=_cparams(("parallel", "parallel", "arbitrary")),
        name="attn_" + mode,
    )(sink, q, k, v, k, v)


def _proj_res_kernel(a_ref, w_ref, b_ref, gt_ref, x_ref, o_ref):
    y = _dot(a_ref[...], w_ref[...]) + b_ref[...]
    o_ref[...] = x_ref[...] + gt_ref[0] * y


def _proj_residual(cfg, a, w, bias, mod3, gate_k, x, n_rows):
    D, tm = cfg.D, cfg.TM
    K = a.shape[1]
    return pl.pallas_call(
        _proj_res_kernel,
        grid=(n_rows // tm,),
        in_specs=[pl.BlockSpec((tm, K), lambda i: (i, 0)),
                  pl.BlockSpec((K, D), lambda i: (0, 0)),
                  pl.BlockSpec((1, D), lambda i: (0, 0)),
                  _mod_spec(cfg, tm, gate_k),
                  pl.BlockSpec((tm, D), lambda i: (i, 0))],
        out_specs=pl.BlockSpec((tm, D), lambda i: (i, 0)),
        out_shape=jax.ShapeDtypeStruct((n_rows, D), F32),
        compiler_params=_cparams(("parallel",)),
        name="proj_residual",
    )(a, w, bias, mod3, x)


def _ffn_kernel(x_ref, sh_ref, sc_ref, gt_ref, g_ref, wg_ref, wu_ref, wd_ref, o_ref, h_scr, acc_scr):
    j = pl.program_id(1)

    @pl.when(j == 0)
    def _():
        h_scr[...] = _modnorm(x_ref[...], g_ref[...], sh_ref[0], sc_ref[0]).astype(BF16)
        acc_scr[...] = jnp.zeros_like(acc_scr)

    h = h_scr[...]
    hid = _silu(_dot(h, wg_ref[...])) * _dot(h, wu_ref[...])
    acc_scr[...] += _dot(hid.astype(BF16), wd_ref[...])

    @pl.when(j == pl.num_programs(1) - 1)
    def _():
        o_ref[...] = x_ref[...] + gt_ref[0] * acc_scr[...]


def _dense_ffn(cfg, x, mod3, g, wg, wu, wd, n_rows):
    D, tm, tf = cfg.D, cfg.TM_FFN, cfg.TF
    ff = wg.shape[1]
    return pl.pallas_call(
        _ffn_kernel,
        grid=(n_rows // tm, ff // tf),
        in_specs=[pl.BlockSpec((tm, D), lambda i, j: (i, 0)),
                  _mod_spec(cfg, tm, 3), _mod_spec(cfg, tm, 4), _mod_spec(cfg, tm, 5),
                  pl.BlockSpec((1, D), lambda i, j: (0, 0)),
                  pl.BlockSpec((D, tf), lambda i, j: (0, j)),
                  pl.BlockSpec((D, tf), lambda i, j: (0, j)),
                  pl.BlockSpec((tf, D), lambda i, j: (j, 0))],
        out_specs=pl.BlockSpec((tm, D), lambda i, j: (i, 0)),
        out_shape=jax.ShapeDtypeStruct((n_rows, D), F32),
        scratch_shapes=[pltpu.VMEM((tm, D), BF16), pltpu.VMEM((tm, D), F32)],
        compiler_params=_cparams(("parallel", "arbitrary")),
        name="dense_ffn",
    )(x, mod3, mod3, mod3, g, wg, wu, wd)


def _router_kernel(x_ref, sh_ref, sc_ref, g_ref, r_ref, h_ref, idx_ref, wt_ref):
    h = _modnorm(x_ref[...], g_ref[...], sh_ref[0], sc_ref[0])
    h_ref[...] = h
    logits = _dot3(h, r_ref[...])
    lane = lax.broadcasted_iota(I32, logits.shape, 1)
    lg = jnp.where(lane < N_EXPERTS, logits, -1e30)
    m1 = jnp.max(lg, axis=-1, keepdims=True)
    i1 = jnp.min(jnp.where(lg == m1, lane, LANES), axis=-1, keepdims=True)
    lg2 = jnp.where(lane == i1, -1e30, lg)
    m2 = jnp.max(lg2, axis=-1, keepdims=True)
    i2 = jnp.min(jnp.where(lg2 == m2, lane, LANES), axis=-1, keepdims=True)
    e = jnp.exp(m2 - m1)
    w1 = 1.0 / (1.0 + e)
    w2 = e / (1.0 + e)
    idx_ref[...] = jnp.where(lane == 0, i1, jnp.where(lane == 1, i2, 0))
    wt_ref[...] = jnp.where(lane == 0, w1, jnp.where(lane == 1, w2, 0.0))


def _router(cfg, x, mod3, g, router_pad, n_rows):
    D, tm = cfg.D, cfg.TM
    return pl.pallas_call(
        _router_kernel,
        grid=(n_rows // tm,),
        in_specs=[pl.BlockSpec((tm, D), lambda i: (i, 0)),
                  _mod_spec(cfg, tm, 3), _mod_spec(cfg, tm, 4),
                  pl.BlockSpec((1, D), lambda i: (0, 0)),
                  pl.BlockSpec((D, LANES), lambda i: (0, 0))],
        out_specs=[pl.BlockSpec((tm, D), lambda i: (i, 0)),
                   pl.BlockSpec((tm, LANES), lambda i: (i, 0)),
                   pl.BlockSpec((tm, LANES), lambda i: (i, 0))],
        out_shape=[jax.ShapeDtypeStruct((n_rows, D), F32),
                   jax.ShapeDtypeStruct((n_rows, LANES), I32),
                   jax.ShapeDtypeStruct((n_rows, LANES), F32)],
        compiler_params=_cparams(("parallel",)),
        name="moe_router",
    )(x, mod3, mod3, g, router_pad)


def _moe_plan(cfg, idx, n_rows):
    ts = cfg.TS
    A = 2 * n_rows
    S = A + N_EXPERTS * ts
    e = idx[:, :2].reshape(A)
    onehot = (e[:, None] == jnp.arange(N_EXPERTS, dtype=I32)[None, :]).astype(I32)
    csum = jnp.cumsum(onehot, axis=0)
    rank = jnp.sum((csum - 1) * onehot, axis=1)
    counts = csum[-1]
    padded = ((counts + ts - 1) // ts) * ts
    ends = jnp.cumsum(padded)
    starts = ends - padded
    slot = starts[e] + rank
    seg_size = jnp.concatenate([padded - counts, (S - ends[-1]).reshape(1)])
    seg_start = jnp.concatenate([starts + counts, ends[-1:]])
    seg_end = jnp.cumsum(seg_size)
    p = jnp.arange(S - A, dtype=I32)
    seg = jnp.sum((p[:, None] >= seg_end[None, :]).astype(I32), axis=1)
    pad_slot = seg_start[seg] + p - (seg_end - seg_size)[seg]
    tile_e = jnp.minimum(jnp.searchsorted(ends, jnp.arange(S // ts, dtype=I32) * ts, side="right"),
                         N_EXPERTS - 1).astype(I32)
    n_used = (ends[-1] // ts).astype(I32).reshape(1)
    return jnp.concatenate([slot, pad_slot]).astype(I32), tile_e, n_used


def _dispatch_kernel(slot_ref, h_hbm, xs_hbm, sem, *, per_step, n_assign):
    base = pl.program_id(0) * per_step

    def body(k, carry):
        a = base + k
        tok = jnp.where(a < n_assign, a // 2, 0)
        pltpu.make_async_copy(h_hbm.at[pl.ds(tok, 1)], xs_hbm.at[pl.ds(slot_ref[k], 1)], sem).start()
        return carry

    lax.fori_loop(0, per_step, body, 0, unroll=8)
    pltpu.make_async_copy(h_hbm.at[pl.ds(0, per_step)], xs_hbm.at[pl.ds(0, per_step)], sem).wait()


def _moe_dispatch(cfg, h, all_slot, n_rows):
    D, per_step = cfg.D, cfg.TD
    S = all_slot.shape[0]
    return pl.pallas_call(
        functools.partial(_dispatch_kernel, per_step=per_step, n_assign=2 * n_rows),
        grid=(S // per_step,),
        in_specs=[pl.BlockSpec((per_step,), lambda i: (i,), memory_space=pltpu.SMEM),
                  pl.BlockSpec(memory_space=pl.ANY)],
        out_specs=pl.BlockSpec(memory_space=pl.ANY),
        out_shape=jax.ShapeDtypeStruct((S, D), F32),
        scratch_shapes=[pltpu.SemaphoreType.DMA(())],
        compiler_params=_cparams(("arbitrary",)),
        name="moe_dispatch",
    )(all_slot, h)


def _expert_up_kernel(te_ref, nu_ref, x_ref, wg_ref, wu_ref, o_ref, wg_scr, wu_scr):
    i = pl.program_id(1)

    @pl.when(jnp.logical_or(i == 0, te_ref[i] != te_ref[jnp.maximum(i - 1, 0)]))
    def _():
        wg_scr[...] = wg_ref[0, 0].astype(BF16)
        wu_scr[...] = wu_ref[0, 0].astype(BF16)

    used = i < nu_ref[0]

    @pl.when(used)
    def _():
        x = x_ref[...].astype(BF16)
        o_ref[...] = (_silu(_dot(x, wg_scr[...])) * _dot(x, wu_scr[...])).astype(BF16)

    @pl.when(jnp.logical_not(used))
    def _():
        o_ref[...] = jnp.zeros_like(o_ref)


def _expert_up(cfg, xs, wg, wu, layer, tile_e, n_used):
    D, ts, tf = cfg.D, cfg.TS, cfg.TFE
    S, ff = xs.shape[0], wg.shape[3]
    return pl.pallas_call(
        _expert_up_kernel,
        grid_spec=pltpu.PrefetchScalarGridSpec(
            num_scalar_prefetch=2, grid=(ff // tf, S // ts),
            in_specs=[pl.BlockSpec((ts, D), lambda j, i, *_: (i, 0)),
                      pl.BlockSpec((1, 1, D, tf), lambda j, i, te, nu: (layer, te[i], 0, j)),
                      pl.BlockSpec((1, 1, D, tf), lambda j, i, te, nu: (layer, te[i], 0, j))],
            out_specs=pl.BlockSpec((ts, tf), lambda j, i, *_: (i, j)),
            scratch_shapes=[pltpu.VMEM((D, tf), BF16), pltpu.VMEM((D, tf), BF16)]),
        out_shape=jax.ShapeDtypeStruct((S, ff), BF16),
        compiler_params=_cparams(("arbitrary", "arbitrary")),
        name="expert_up",
    )(tile_e, n_used, xs, wg, wu)


def _expert_down_kernel(te_ref, nu_ref, h_ref, wd_ref, y_ref, wd_scr):
    i = pl.program_id(1)

    @pl.when(jnp.logical_or(i == 0, te_ref[i] != te_ref[jnp.maximum(i - 1, 0)]))
    def _():
        wd_scr[...] = wd_ref[0, 0].astype(BF16)

    used = i < nu_ref[0]

    @pl.when(used)
    def _():
        y_ref[...] = _dot(h_ref[...], wd_scr[...])

    @pl.when(jnp.logical_not(used))
    def _():
        y_ref[...] = jnp.zeros_like(y_ref)


def _expert_down(cfg, hs, wd, layer, tile_e, n_used):
    D, ts, tn = cfg.D, cfg.TS, cfg.TND
    S, ff = hs.shape
    return pl.pallas_call(
        _expert_down_kernel,
        grid_spec=pltpu.PrefetchScalarGridSpec(
            num_scalar_prefetch=2, grid=(D // tn, S // ts),
            in_specs=[pl.BlockSpec((ts, ff), lambda n, i, *_: (i, 0)),
                      pl.BlockSpec((1, 1, ff, tn), lambda n, i, te, nu: (layer, te[i], 0, n))],
            out_specs=pl.BlockSpec((ts, tn), lambda n, i, *_: (i, n)),
            scratch_shapes=[pltpu.VMEM((ff, tn), BF16)]),
        out_shape=jax.ShapeDtypeStruct((S, D), F32),
        compiler_params=_cparams(("arbitrary", "arbitrary")),
        name="expert_down",
    )(tile_e, n_used, hs, wd)


def _combine_kernel(slot_ref, y_hbm, x_ref, wt_ref, gt_ref, o_ref, ybuf, sem):
    tm = x_ref.shape[0]

    def body(r, carry):
        for k in range(2):
            pltpu.make_async_copy(y_hbm.at[pl.ds(slot_ref[2 * r + k], 1)], ybuf.at[k, pl.ds(r, 1)], sem).start()
        return carry

    lax.fori_loop(0, tm, body, 0, unroll=4)
    for k in range(2):
        pltpu.make_async_copy(y_hbm.at[pl.ds(0, tm)], ybuf.at[k], sem).wait()
    wt = wt_ref[...]
    moe = wt[:, 0:1] * ybuf[0] + wt[:, 1:2] * ybuf[1]
    o_ref[...] = x_ref[...] + gt_ref[0] * moe


def _moe_combine(cfg, x, y, slot, wt, mod3, n_rows):
    D, tm = cfg.D, cfg.TMC
    return pl.pallas_call(
        _combine_kernel,
        grid=(n_rows // tm,),
        in_specs=[pl.BlockSpec((2 * tm,), lambda i: (i,), memory_space=pltpu.SMEM),
                  pl.BlockSpec(memory_space=pl.ANY),
                  pl.BlockSpec((tm, D), lambda i: (i, 0)),
                  pl.BlockSpec((tm, LANES), lambda i: (i, 0)),
                  _mod_spec(cfg, tm, 5)],
        out_specs=pl.BlockSpec((tm, D), lambda i: (i, 0)),
        out_shape=jax.ShapeDtypeStruct((n_rows, D), F32),
        scratch_shapes=[pltpu.VMEM((2, tm, D), F32), pltpu.SemaphoreType.DMA(())],
        compiler_params=_cparams(("arbitrary",)),
        name="moe_combine",
    )(slot, y, x, wt, mod3)


def _moe_layer(cfg, x, mod3, g, router_pad, wg, wu, wd, layer, n_rows):
    h, idx, wt = _router(cfg, x, mod3, g, router_pad, n_rows)
    all_slot, tile_e, n_used = _moe_plan(cfg, idx, n_rows)
    xs = _moe_dispatch(cfg, h, all_slot, n_rows)
    hs = _expert_up(cfg, xs, wg, wu, layer, tile_e, n_used)
    y = _expert_down(cfg, hs, wd, layer, tile_e, n_used)
    return _moe_combine(cfg, x, y, all_slot[:2 * n_rows], wt, mod3, n_rows)


def _dft_matrix(n_half):
    n = 2 * n_half
    nb = 64
    r = jnp.arange(n, dtype=I32)
    is_cos = (r <= n_half)[:, None]
    k = jnp.where(r <= n_half, r, r - n_half)[:, None]
    ang_a = ((k * (jnp.arange(n_half // nb, dtype=I32) * nb)[None, :]) % n).astype(F32) * (2.0 * np.pi / n)
    ang_b = ((k * jnp.arange(nb, dtype=I32)[None, :]) % n).astype(F32) * (2.0 * np.pi / n)
    ca = jnp.where(is_cos, jnp.cos(ang_a), jnp.sin(ang_a))[:, :, None]
    sa = jnp.where(is_cos, jnp.sin(ang_a), -jnp.cos(ang_a))[:, :, None]
    cb, sb = jnp.cos(ang_b)[:, None, :], jnp.sin(ang_b)[:, None, :]
    return (ca * cb - sa * sb).reshape(n, n_half)


def _hy_filter_kernel(z_ref, w1_ref, b1_ref, w2_ref, b2_ref, w3f_ref, w3b_ref, f_ref, df_ref, db_ref,
                      ft_ref, fb_ref, tp_ref, ga_ref, gb_ref, gc_ref, sum_scr, dif_scr, nrm_scr, nyq_scr, *, seq):
    i = pl.program_id(1)
    tm = ft_ref.shape[0]
    n = 2 * seq

    @pl.when(i == 0)
    def _():
        z = z_ref[...]
        a = jnp.sin(f_ref[0:1, :] * (_dot3(z, w1_ref[...]) + b1_ref[...]))
        a = jnp.sin(f_ref[1:2, :] * (_dot3(a, w2_ref[...]) + b2_ref[...]))
        t_unit = z[:, 0:1]
        row = lax.broadcasted_iota(I32, (seq, 1), 0)
        hf = _dot3(a, w3f_ref[...]) * jnp.exp(-t_unit * jnp.exp(df_ref[...]))
        hb = _dot3(a, w3b_ref[...]) * jnp.exp(-t_unit * jnp.exp(db_ref[...]))
        hb = jnp.where(row == 0, 0.0, hb)
        nrm_scr[...] = jnp.sum(jnp.abs(hf) + jnp.abs(hb), axis=0, keepdims=True)
        tot = hf + hb
        sum_scr[...] = tot.astype(BF16)
        dif_scr[...] = (hf - hb).astype(BF16)
        nyq_scr[...] = jnp.sum(jnp.where((row & 1) == 0, tot, -tot), axis=0, keepdims=True)

    p = _dot(ft_ref[...], sum_scr[...])
    q = _dot(fb_ref[...], dif_scr[...])
    inv = 1.0 / nrm_scr[...]
    first = (i * tm + lax.broadcasted_iota(I32, (tm, 1), 0)) == 0
    w_one, w_two = inv * (1.0 / n), inv * (2.0 / n)
    wt = jnp.where(first, w_one, w_two)
    tp_ref[...] = wt * p
    ga_ref[...] = jnp.where(first, 0.0, -(w_two * q))
    gb_ref[...] = jnp.where(first, 0.0, w_two * q)
    gc_ref[...] = jnp.where(first, w_one * nyq_scr[...], w_two * p)


def _pad_to(a, shape):
    return jnp.zeros(shape, a.dtype).at[tuple(slice(0, d) for d in a.shape)].set(a)


def _hy_filter(cfg, seq, z, w1, b1, w2, b2, w3, sin_freq, log_decay, fmat):
    D, tc = cfg.D, cfg.TC
    tm = min(cfg.TMF, seq)
    pw = LANES
    full = lambda shp: pl.BlockSpec(shp, lambda j, i: (0,) * len(shp))
    col = lambda rows, off: pl.BlockSpec((rows, tc), lambda j, i: (0, off + j))
    out_spec = pl.BlockSpec((tm, tc), lambda j, i: (i, j))
    w3p = _pad_to(w3, (pw, 2 * D))
    return pl.pallas_call(
        functools.partial(_hy_filter_kernel, seq=seq),
        grid=(D // tc, seq // tm),
        in_specs=[full((seq, pw)), full((pw, pw)), full((1, pw)), full((pw, pw)), full((1, pw)),
                  col(pw, 0), col(pw, D // tc), full((2, pw)), col(1, 0), col(1, D // tc),
                  pl.BlockSpec((tm, seq), lambda j, i: (i, 0)),
                  pl.BlockSpec((tm, seq), lambda j, i: (seq // tm + i, 0))],
        out_specs=[out_spec] * 4,
        out_shape=[jax.ShapeDtypeStruct((seq, D), F32)] * 4,
        scratch_shapes=[pltpu.VMEM((seq, tc), BF16), pltpu.VMEM((seq, tc), BF16),
                        pltpu.VMEM((1, tc), F32), pltpu.VMEM((1, tc), F32)],
        compiler_params=_cparams(("arbitrary", "arbitrary")),
        name="hyena_filter",
    )(_pad_to(z, (seq, pw)), _pad_to(w1, (pw, pw)), _pad_to(b1.reshape(1, -1), (1, pw)),
      _pad_to(w2, (pw, pw)), _pad_to(b2.reshape(1, -1), (1, pw)), w3p, w3p, _pad_to(sin_freq, (2, pw)),
      log_decay.reshape(1, 2 * D), log_decay.reshape(1, 2 * D), fmat, fmat)


def _hy_in_kernel(x_ref, sh_ref, sc_ref, g_ref, w0_ref, w1_ref, w2_ref, b_ref, cw_ref, cb_ref,
                  x0_ref, u_ref, ub_ref, h_scr):
    j = pl.program_id(1)
    seq = x_ref.shape[0]

    @pl.when(j == 0)
    def _():
        h_scr[...] = _modnorm(x_ref[...], g_ref[...], sh_ref[0], sc_ref[0]).astype(BF16)

    h = h_scr[...]
    row = lax.broadcasted_iota(I32, (seq, 1), 0)

    def branch(w_ref, t):
        p = _dot(h, w_ref[...]) + b_ref[t]
        prev = jnp.where(row == 0, 0.0, pltpu.roll(p, 1, 0))
        nxt = jnp.where(row == seq - 1, 0.0, pltpu.roll(p, seq - 1, 0))
        cw = cw_ref[t]
        return cw[0:1] * prev + cw[1:2] * p + cw[2:3] * nxt + cb_ref[t]

    x0_ref[...] = branch(w0_ref, 0)
    u = branch(w1_ref, 1) * branch(w2_ref, 2)
    u_ref[...] = u
    ub_ref[...] = u.astype(BF16)


def _hy_in(cfg, seq, row_blk0, x, mod3, g, w, b3, cw3, cb3):
    D, tc = cfg.D, cfg.TC
    nj = D // tc
    rows = cfg.B * seq
    mrow = (lambda b: b) if row_blk0 == 0 else (lambda b: cfg.B)
    mod = lambda k: pl.BlockSpec((1, 1, D), lambda b, j: (mrow(b) * 6 + k, 0, 0))
    wspec = lambda t: pl.BlockSpec((D, tc), lambda b, j: (0, t * nj + j))
    out = pl.BlockSpec((seq, tc), lambda b, j: (b, j))
    return pl.pallas_call(
        _hy_in_kernel,
        grid=(cfg.B, nj),
        in_specs=[pl.BlockSpec((seq, D), lambda b, j: (row_blk0 + b, 0)), mod(0), mod(1),
                  pl.BlockSpec((1, D), lambda b, j: (0, 0)),
                  wspec(0), wspec(1), wspec(2),
                  pl.BlockSpec((3, 1, tc), lambda b, j: (0, 0, j)),
                  pl.BlockSpec((3, 3, tc), lambda b, j: (0, 0, j)),
                  pl.BlockSpec((3, 1, tc), lambda b, j: (0, 0, j))],
        out_specs=[out, out, out],
        out_shape=[jax.ShapeDtypeStruct((rows, D), F32), jax.ShapeDtypeStruct((rows, D), F32),
                   jax.ShapeDtypeStruct((rows, D), BF16)],
        scratch_shapes=[pltpu.VMEM((seq, D), BF16)],
        compiler_params=_cparams(("parallel", "arbitrary")),
        name="hyena_in",
    )(x, mod3, mod3, g, w, w, w, b3, cw3, cb3)


def _hy_fwd_kernel(ft_ref, fb_ref, u_ref, tp_ref, ga_ref, gb_ref, gc_ref, zt_ref, zb_ref):
    u = u_ref[...]
    a = _dot(ft_ref[...], u)
    b = _dot(fb_ref[...], u)
    zt_ref[0] = (a * tp_ref[...] + b * ga_ref[...]).astype(BF16)
    zb_ref[0] = (a * gb_ref[...] + b * gc_ref[...]).astype(BF16)


def _hy_fwd(cfg, seq, ub, fmat, filt):
    D, tc = cfg.D, cfg.TC
    tm = min(cfg.TMF, seq)
    nt = seq // tm
    g_spec = pl.BlockSpec((tm, tc), lambda b, j, i: (i, j))
    return pl.pallas_call(
        _hy_fwd_kernel,
        grid=(cfg.B, D // tc, nt),
        in_specs=[pl.BlockSpec((tm, seq), lambda b, j, i: (i, 0)),
                  pl.BlockSpec((tm, seq), lambda b, j, i: (nt + i, 0)),
                  pl.BlockSpec((seq, tc), lambda b, j, i: (b, j)),
                  g_spec, g_spec, g_spec, g_spec],
        out_specs=[pl.BlockSpec((1, tm, tc), lambda b, j, i: (b, i, j))] * 2,
        out_shape=[jax.ShapeDtypeStruct((cfg.B, seq, D), BF16)] * 2,
        compiler_params=_cparams(("parallel", "parallel", "arbitrary")),
        name="hyena_fwd_dft",
    )(fmat, fmat, ub, *filt)


def _hy_inv_kernel(fi_ref, zt_ref, zb_ref, u_ref, x0_ref, skip_ref, y_ref, *, seq):
    fi = fi_ref[...]
    conv = _dot(fi[:, :seq], zt_ref[0]) + _dot(fi[:, seq:], zb_ref[0])
    v = conv + skip_ref[...] * u_ref[...]
    y_ref[...] = (x0_ref[...] * v).astype(BF16)


def _hy_inv(cfg, seq, zt, zb, finv, u, x0, skip):
    D, tc = cfg.D, cfg.TC
    tm = min(cfg.TMF, seq)
    io = pl.BlockSpec((tm, tc), lambda b, j, i: (b * (seq // tm) + i, j))
    z_spec = pl.BlockSpec((1, seq, tc), lambda b, j, i: (b, 0, j))
    return pl.pallas_call(
        functools.partial(_hy_inv_kernel, seq=seq),
        grid=(cfg.B, D // tc, seq // tm),
        in_specs=[pl.BlockSpec((tm, 2 * seq), lambda b, j, i: (i, 0)), z_spec, z_spec, io, io,
                  pl.BlockSpec((1, tc), lambda b, j, i: (0, j))],
        out_specs=io,
        out_shape=jax.ShapeDtypeStruct((cfg.B * seq, D), BF16),
        compiler_params=_cparams(("parallel", "parallel", "arbitrary")),
        name="hyena_inv_dft",
    )(finv, zt, zb, u, x0, skip)


def _hy_positions(seq):
    t = jnp.arange(seq, dtype=F32)
    t_unit = t / max(seq - 1, 1)
    phase = 2.0 * np.pi * t / seq
    bands = jnp.linspace(1e-4, HY_BANDS - 1, HY_BANDS, dtype=F32)
    ang = phase[:, None] * bands[None, :]
    return jnp.concatenate([t_unit[:, None], jnp.cos(ang), -jnp.sin(ang)], axis=-1)


def _hyena_seq(cfg, seq, row_blk0, x, mod3, g, w_in, b3, cw3, cb3, filt_params, skip):
    fmat = _dft_matrix(seq)
    fb16 = fmat.astype(BF16)
    filt = _hy_filter(cfg, seq, _hy_positions(seq), *filt_params, fb16)
    x0, u, ub = _hy_in(cfg, seq, row_blk0, x, mod3, g, w_in, b3, cw3, cb3)
    zt, zb = _hy_fwd(cfg, seq, ub, fb16, filt)
    return _hy_inv(cfg, seq, zt, zb, fmat.T.astype(BF16), u, x0, skip)


def _final_norm_kernel(x_ref, g_ref, o_ref):
    x = x_ref[...]
    o_ref[...] = (x * lax.rsqrt(jnp.mean(x * x, axis=-1, keepdims=True) + NORM_EPS)) * g_ref[...]


def _final_norm(cfg, x, g):
    D, tm = cfg.D, cfg.TM
    return pl.pallas_call(
        _final_norm_kernel,
        grid=(cfg.ML // tm,),
        in_specs=[pl.BlockSpec((tm, D), lambda i: (i, 0)), pl.BlockSpec((1, D), lambda i: (0, 0))],
        out_specs=pl.BlockSpec((tm, D), lambda i: (i, 0)),
        out_shape=jax.ShapeDtypeStruct((cfg.ML, D), F32),
        compiler_params=_cparams(("parallel",)),
        name="final_norm",
    )(x, g)


def _rope_tables(cfg):
    L, tm = cfg.L, cfg.TM
    t = jnp.arange(L, dtype=I32)
    row, col = (t // cfg.GRID_W).astype(F32), (t % cfg.GRID_W).astype(F32)
    inv = ROPE_BASE ** (-jnp.arange(ROPE_FREQS, dtype=F32) / ROPE_FREQS)
    a0, a1 = row[:, None] * inv, col[:, None] * inv
    cos_h = jnp.concatenate([jnp.cos(a0), jnp.cos(a0), jnp.cos(a1), jnp.cos(a1)], axis=1)
    sin_h = jnp.concatenate([-jnp.sin(a0), jnp.sin(a0), -jnp.sin(a1), jnp.sin(a1)], axis=1)
    cos_t = jnp.concatenate([jnp.tile(cos_h, (1, 2)), jnp.ones((tm, LANES), F32)], axis=0)
    sin_t = jnp.concatenate([jnp.tile(sin_h, (1, 2)), jnp.zeros((tm, LANES), F32)], axis=0)
    return cos_t, sin_t


def _head_perm(D):
    heads = []
    for hp in range(D // (2 * GROUP * HEAD_DIM)):
        for c in range(GROUP):
            heads += [2 * GROUP * hp + c, 2 * GROUP * hp + GROUP + c]
    return np.concatenate([np.arange(h * HEAD_DIM, (h + 1) * HEAD_DIM) for h in heads])


def _attn_layer(cfg, mode, last, x, mod3, g, w_in, w_out, q_g, k_g, sink, tables):
    D = cfg.D
    perm = _head_perm(D)
    w = jnp.concatenate([w_in[:, :D][:, perm], w_in[:, D:]], axis=1).astype(BF16)
    wo = w_out[perm, :].astype(BF16)
    qg = jnp.tile(q_g * (HEAD_DIM ** -0.5), 2).reshape(1, LANES)
    kg = jnp.tile(k_g, 2).reshape(1, LANES)
    blk = np.kron(np.eye(2, dtype=np.float32), np.ones((HEAD_DIM, HEAD_DIM), np.float32))
    cos_t, sin_t = tables
    q, k, v = _qkv_proj(cfg, x, mod3, g, w, qg, kg, cos_t, sin_t, jnp.asarray(blk, BF16))
    use_sink = sink is not None
    sink = sink.astype(F32) if use_sink else jnp.zeros((D // HEAD_DIM,), F32)
    o = _attention(cfg, q, k, v, sink, mode, use_sink, not last)
    n_rows = cfg.ML if last else cfg.MT
    return _proj_residual(cfg, o, wo, jnp.zeros((1, D), F32), mod3, 2, x, n_rows)


def _hyena_layer(cfg, last, x, mod3, g, w_in, b_in, conv_w, conv_b, filt_params, skip, w_out, b_out):
    D = cfg.D
    w = w_in.astype(BF16)
    b3, cb3 = b_in.reshape(3, 1, D), conv_b.reshape(3, 1, D)
    cw3 = conv_w.reshape(3, 3, D).transpose(1, 0, 2)
    skip = skip.reshape(1, D)
    y = _hyena_seq(cfg, cfg.L, 0, x, mod3, g, w, b3, cw3, cb3, filt_params, skip)
    n_rows = cfg.ML
    if not last:
        yc = _hyena_seq(cfg, cfg.LC, cfg.ML // cfg.LC, x, mod3, g, w, b3, cw3, cb3, filt_params, skip)
        y = jnp.concatenate([y, yc], axis=0)
        n_rows = cfg.MT
    return _proj_residual(cfg, y, w_out.astype(BF16), b_out.reshape(1, D), mod3, 2, x, n_rows)


def _forward(cfg, x, c, ctx, c_ctx, mod_w, mod_b, norm_g,
             gattn_w_in, gattn_w_out, gattn_q_norm, gattn_k_norm,
             wattn_w_in, wattn_w_out, wattn_q_norm, wattn_k_norm, wattn_sink,
             hy_w_in, hy_b_in, hy_conv_w, hy_conv_b, hy_ffn_w1, hy_ffn_b1, hy_ffn_w2, hy_ffn_b2,
             hy_ffn_w3, hy_sin_freq, hy_log_decay, hy_skip, hy_w_out, hy_b_out,
             ffn_w_gate, ffn_w_up, ffn_w_down,
             moe_router, moe_w_gate, moe_w_up, moe_w_down, final_norm_g):
    D, depth = cfg.D, cfg.DEPTH
    xs = jnp.concatenate([x.reshape(cfg.ML, D), ctx.reshape(cfg.MC, D)], axis=0)
    mod = _mod_vectors(cfg, c, c_ctx, mod_w, mod_b)
    tables = _rope_tables(cfg)
    for i in range(depth):
        last = i == depth - 1
        mod3 = mod[i]
        kind, j = i % 3, i // 3
        g_a, g_f = norm_g[i, 0].reshape(1, D), norm_g[i, 1].reshape(1, D)
        if kind == 0:
            xs = _attn_layer(cfg, "global", last, xs, mod3, g_a, gattn_w_in[j], gattn_w_out[j],
                             gattn_q_norm[j], gattn_k_norm[j], None, tables)
        elif kind == 1:
            xs = _attn_layer(cfg, "window", last, xs, mod3, g_a, wattn_w_in[j], wattn_w_out[j],
                             wattn_q_norm[j], wattn_k_norm[j], wattn_sink[j], tables)
        else:
            filt = (hy_ffn_w1[j], hy_ffn_b1[j], hy_ffn_w2[j], hy_ffn_b2[j], hy_ffn_w3[j],
                    hy_sin_freq[j], hy_log_decay[j])
            xs = _hyena_layer(cfg, last, xs, mod3, g_a, hy_w_in[j], hy_b_in[j], hy_conv_w[j], hy_conv_b[j],
                              filt, hy_skip[j], hy_w_out[j], hy_b_out[j])
        n_rows = cfg.ML if last else cfg.MT
        k_ff = i // 2
        if i % 2 == 0:
            xs = _dense_ffn(cfg, xs, mod3, g_f, ffn_w_gate[k_ff].astype(BF16), ffn_w_up[k_ff].astype(BF16),
                            ffn_w_down[k_ff].astype(BF16), n_rows)
        else:
            router_pad = jnp.zeros((D, LANES), F32).at[:, :N_EXPERTS].set(moe_router[k_ff])
            xs = _moe_layer(cfg, xs, mod3, g_f, router_pad, moe_w_gate, moe_w_up, moe_w_down, k_ff, n_rows)
    return _final_norm(cfg, xs, final_norm_g.reshape(1, D)).reshape(cfg.B, cfg.L, D)


def kernel(x, c, ctx, c_ctx, mod_w, mod_b, norm_g, gattn_w_in, gattn_w_out, gattn_q_norm, gattn_k_norm, wattn_w_in, wattn_w_out, wattn_q_norm, wattn_k_norm, wattn_sink, hy_w_in, hy_b_in, hy_conv_w, hy_conv_b, hy_ffn_w1, hy_ffn_b1, hy_ffn_w2, hy_ffn_b2, hy_ffn_w3, hy_sin_freq, hy_log_decay, hy_skip, hy_w_out, hy_b_out, ffn_w_gate, ffn_w_up, ffn_w_down, moe_router, moe_w_gate, moe_w_up, moe_w_down, final_norm_g):
    return _forward(Cfg(), x, c, ctx, c_ctx, mod_w, mod_b, norm_g, gattn_w_in, gattn_w_out, gattn_q_norm,
                    gattn_k_norm, wattn_w_in, wattn_w_out, wattn_q_norm, wattn_k_norm, wattn_sink, hy_w_in,
                    hy_b_in, hy_conv_w, hy_conv_b, hy_ffn_w1, hy_ffn_b1, hy_ffn_w2, hy_ffn_b2, hy_ffn_w3,
                    hy_sin_freq, hy_log_decay, hy_skip, hy_w_out, hy_b_out, ffn_w_gate, ffn_w_up, ffn_w_down,
                    moe_router, moe_w_gate, moe_w_up, moe_w_down, final_norm_g)
```

```python
import functools
import math
from typing import NamedTuple

import numpy as np
import jax
import jax.numpy as jnp
from jax import lax
from jax.experimental import pallas as pl
from jax.experimental.pallas import tpu as pltpu

F32, BF16, I32 = jnp.float32, jnp.bfloat16, jnp.int32

HEAD_DIM = 64
GROUP = 4
ROPE_BASE = 10000.0
ROPE_FREQS = HEAD_DIM // 4
HY_BANDS = 16
NORM_EPS = 1e-6
N_EXPERTS = 8
LANES = 128
VMEM_LIMIT = 56 * 1024 * 1024


class Cfg(NamedTuple):
    B: int = 8
    L: int = 2048
    LC: int = 256
    D: int = 1024
    GRID_W: int = 64
    WINDOW: int = 128
    DEPTH: int = 4
    TM: int = 512
    TM_FFN: int = 1024
    TF: int = 256
    TQ: int = 128
    TK: int = 512
    TS: int = 512
    TD: int = 1024
    TMC: int = 512
    TFE: int = 512
    TND: int = 512
    TC: int = 256
    TMF: int = 512

    @property
    def ML(self):
        return self.B * self.L

    @property
    def MC(self):
        return self.B * self.LC

    @property
    def MT(self):
        return self.ML + self.MC


def _cparams(sem):
    return pltpu.CompilerParams(dimension_semantics=sem, vmem_limit_bytes=VMEM_LIMIT)


def _dot(a, b):
    return jnp.dot(a, b, preferred_element_type=F32)


def _dot_nt(a, b):
    return lax.dot_general(a, b, (((1,), (1,)), ((), ())), preferred_element_type=F32)


def _split(x):
    hi = x.astype(BF16)
    return hi, (x - hi.astype(F32)).astype(BF16)


def _dot3(a, b):
    ah, al = _split(a)
    bh, bl = _split(b)
    return _dot(ah, bh) + (_dot(al, bh) + _dot(ah, bl))


def _silu(a):
    return a * jax.nn.sigmoid(a)


def _modnorm(x, g, shift, scale):
    y = x * lax.rsqrt(jnp.mean(x * x, axis=-1, keepdims=True) + NORM_EPS)
    return (y * g) * (1.0 + scale) + shift


def _mod_row(cfg, i, tm):
    return jnp.minimum((i * tm) // cfg.L, cfg.B)


def _mod_spec(cfg, tm, k):
    D = cfg.D
    return pl.BlockSpec((1, 1, D), lambda i, *_: (_mod_row(cfg, i, tm) * 6 + k, 0, 0))


def _mod_kernel(c_ref, w_ref, b_ref, o_ref):
    o_ref[0] = _dot3(_silu(c_ref[...]), w_ref[0]) + b_ref[0]


def _mod_vectors(cfg, c, c_ctx, mod_w, mod_b):
    D, depth = cfg.D, cfg.DEPTH
    tn = 3 * D // 2
    rows = jnp.zeros((16, D), F32).at[:cfg.B].set(c).at[cfg.B].set(c_ctx)
    out = pl.pallas_call(
        _mod_kernel,
        grid=(depth, 6 * D // tn),
        in_specs=[pl.BlockSpec((16, D), lambda l, j: (0, 0)),
                  pl.BlockSpec((1, D, tn), lambda l, j: (l, 0, j)),
                  pl.BlockSpec((1, 1, tn), lambda l, j: (l, 0, j))],
        out_specs=pl.BlockSpec((1, 16, tn), lambda l, j: (l, 0, j)),
        out_shape=jax.ShapeDtypeStruct((depth, 16, 6 * D), F32),
        compiler_params=_cparams(("arbitrary", "arbitrary")),
        name="mod_vectors",
    )(rows, mod_w, mod_b.reshape(depth, 1, 6 * D))
    return out.reshape(depth, 16 * 6, 1, D)


def _qkv_kernel(x_ref, sh_ref, sc_ref, g_ref, w_ref, qg_ref, kg_ref, cos_ref, sin_ref, bd_ref,
                q_ref, k_ref, v_ref, *, attn_dim, kv_dim):
    h = _modnorm(x_ref[...], g_ref[...], sh_ref[0], sc_ref[0])
    p = _dot(h.astype(BF16), w_ref[...])
    tm = p.shape[0]
    upper = (lax.broadcasted_iota(I32, (tm, LANES), 1) & ROPE_FREQS) != 0
    cosv, sinv, bd = cos_ref[...], sin_ref[...], bd_ref[...]

    def norm_rope(xc, gain):
        hi, lo = _split(xc * xc)
        ss = _dot(hi, bd) + _dot(lo, bd)
        y = (xc * lax.rsqrt(ss * (1.0 / HEAD_DIM) + NORM_EPS)) * gain
        partner = jnp.where(upper, pltpu.roll(y, ROPE_FREQS, 1), pltpu.roll(y, LANES - ROPE_FREQS, 1))
        return y * cosv + partner * sinv

    for c in range(attn_dim // LANES):
        sl = slice(c * LANES, (c + 1) * LANES)
        q_ref[:, sl] = norm_rope(p[:, sl], qg_ref[...]).astype(BF16)
    for c in range(kv_dim // LANES):
        sl = slice(c * LANES, (c + 1) * LANES)
        k_ref[:, sl] = norm_rope(p[:, attn_dim + c * LANES: attn_dim + (c + 1) * LANES], kg_ref[...]).astype(BF16)
    v_ref[...] = p[:, attn_dim + kv_dim:].astype(BF16)


def _qkv_proj(cfg, x, mod3, g, w, qg, kg, cos_t, sin_t, bd):
    D, tm = cfg.D, cfg.TM
    kv = D // GROUP
    n_lat, per_seq = cfg.ML // tm, cfg.L // tm
    rope_map = lambda i: (jnp.where(i < n_lat, i % per_seq, per_seq), 0)
    full = lambda shp: pl.BlockSpec(shp, lambda i: (0,) * len(shp))
    return pl.pallas_call(
        functools.partial(_qkv_kernel, attn_dim=D, kv_dim=kv),
        grid=(cfg.MT // tm,),
        in_specs=[pl.BlockSpec((tm, D), lambda i: (i, 0)),
                  _mod_spec(cfg, tm, 0), _mod_spec(cfg, tm, 1),
                  full((1, D)), full((D, D + 2 * kv)), full((1, LANES)), full((1, LANES)),
                  pl.BlockSpec((tm, LANES), rope_map), pl.BlockSpec((tm, LANES), rope_map),
                  full((LANES, LANES))],
        out_specs=[pl.BlockSpec((tm, D), lambda i: (i, 0)),
                   pl.BlockSpec((tm, kv), lambda i: (i, 0)),
                   pl.BlockSpec((tm, kv), lambda i: (i, 0))],
        out_shape=[jax.ShapeDtypeStruct((cfg.MT, D), BF16),
                   jax.ShapeDtypeStruct((cfg.MT, kv), BF16),
                   jax.ShapeDtypeStruct((cfg.MT, kv), BF16)],
        compiler_params=_cparams(("parallel",)),
        name="qkv_proj",
    )(x, mod3, mod3, g, w, qg, kg, cos_t, sin_t, bd)


def _attn_tile(sink_ref, q_ref, kl_ref, vl_ref, kc_ref, vc_ref, o_ref, *, mode, tq, tk, seq, window, use_sink):
    hp, qi = pl.program_id(1), pl.program_id(2)
    rows = GROUP * tq
    low = lax.broadcasted_iota(I32, (tq, LANES), 1) < HEAD_DIM
    chunks = [(kc_ref[...], vc_ref[...], None)]
    if mode == "global":
        chunks += [(kl_ref[c * tk:(c + 1) * tk, :], vl_ref[c * tk:(c + 1) * tk, :], None) for c in range(seq // tk)]
    elif mode == "window":
        span = tq + 2 * window
        start = pl.multiple_of(jnp.clip(qi * tq - window, 0, seq - span), math.gcd(tq, window))
        qpos = qi * tq + (lax.broadcasted_iota(I32, (rows, span), 0) & (tq - 1))
        kpos = start + lax.broadcasted_iota(I32, (rows, span), 1)
        chunks += [(kl_ref[pl.ds(start, span), :], vl_ref[pl.ds(start, span), :], jnp.abs(qpos - kpos) <= window)]
    qs, state = [], []
    for s in range(2):
        keep = low if s == 0 else jnp.logical_not(low)
        qs.append(jnp.concatenate(
            [jnp.where(keep, q_ref[:, c * LANES:(c + 1) * LANES], jnp.zeros((), BF16)) for c in range(GROUP)],
            axis=0))
        if use_sink:
            m0 = jnp.concatenate(
                [jnp.full((tq, 1), sink_ref[hp * 2 * GROUP + s * GROUP + g], F32) for g in range(GROUP)], axis=0)
            state.append((m0, jnp.ones((rows, 1), F32), jnp.zeros((rows, LANES), F32)))
        else:
            state.append((jnp.full((rows, 1), -1e30, F32), jnp.zeros((rows, 1), F32), jnp.zeros((rows, LANES), F32)))
    for keys, vals, mask in chunks:
        for s in range(2):
            m, den, acc = state[s]
            sc = _dot_nt(qs[s], keys)
            if mask is not None:
                sc = jnp.where(mask, sc, -1e30)
            m_new = jnp.maximum(m, jnp.max(sc, axis=-1, keepdims=True))
            alpha = jnp.exp(m - m_new)
            p = jnp.exp(sc - m_new)
            state[s] = (m_new, alpha * den + jnp.sum(p, axis=-1, keepdims=True),
                        alpha * acc + _dot(p.astype(BF16), vals))
    outs = [acc / den for _, den, acc in state]
    for c in range(GROUP):
        o_ref[:, c * LANES:(c + 1) * LANES] = jnp.where(
            low, outs[0][c * tq:(c + 1) * tq], outs[1][c * tq:(c + 1) * tq]).astype(BF16)


def _attn_kernel(*refs, mode, n_lat, with_ctx, **kw):
    if not with_ctx:
        _attn_tile(*refs, mode=mode, **kw)
        return
    qi = pl.program_id(2)
    pl.when(qi < n_lat)(lambda: _attn_tile(*refs, mode=mode, **kw))
    pl.when(qi >= n_lat)(lambda: _attn_tile(*refs, mode="ctx", **kw))


def _attention(cfg, q, k, v, sink, mode, use_sink, with_ctx):
    B, L, LC, D, tq = cfg.B, cfg.L, cfg.LC, cfg.D, cfg.TQ
    pairs = D // (2 * GROUP * HEAD_DIM)
    qw = GROUP * LANES
    n_lat, n_ctx = L // tq, (LC // tq if with_ctx else 0)
    q_map = lambda b, hp, qi, *_: (jnp.where(qi < n_lat, b * n_lat + qi, cfg.ML // tq + b * n_ctx + qi - n_lat), hp)
    ctx_map = lambda b, hp, qi, *_: (cfg.ML // LC + b, hp)
    lat_map = lambda b, hp, qi, *_: (b, hp)
    return pl.pallas_call(
        functools.partial(_attn_kernel, mode=mode, n_lat=n_lat, with_ctx=with_ctx, tq=tq, tk=cfg.TK, seq=L,
                          window=cfg.WINDOW, use_sink=use_sink),
        grid_spec=pltpu.PrefetchScalarGridSpec(
            num_scalar_prefetch=1, grid=(B, pairs, n_lat + n_ctx),
            in_specs=[pl.BlockSpec((tq, qw), q_map),
                      pl.BlockSpec((L, LANES), lat_map), pl.BlockSpec((L, LANES), lat_map),
                      pl.BlockSpec((LC, LANES), ctx_map), pl.BlockSpec((LC, LANES), ctx_map)],
            out_specs=pl.BlockSpec((tq, qw), q_map)),
        out_shape=jax.ShapeDtypeStruct((cfg.MT if with_ctx else cfg.ML, D), BF16),
        compiler_params=_cparams(("parallel", "parallel", "arbitrary")),
        name="attn_" + mode,
    )(sink, q, k, v, k, v)


def _proj_res_kernel(a_ref, w_ref, b_ref, gt_ref, x_ref, o_ref):
    y = _dot(a_ref[...], w_ref[...]) + b_ref[...]
    o_ref[...] = x_ref[...] + gt_ref[0] * y


def _proj_residual(cfg, a, w, bias, mod3, gate_k, x, n_rows):
    D, tm = cfg.D, cfg.TM
    K = a.shape[1]
    return pl.pallas_call(
        _proj_res_kernel,
        grid=(n_rows // tm,),
        in_specs=[pl.BlockSpec((tm, K), lambda i: (i, 0)),
                  pl.BlockSpec((K, D), lambda i: (0, 0)),
                  pl.BlockSpec((1, D), lambda i: (0, 0)),
                  _mod_spec(cfg, tm, gate_k),
                  pl.BlockSpec((tm, D), lambda i: (i, 0))],
        out_specs=pl.BlockSpec((tm, D), lambda i: (i, 0)),
        out_shape=jax.ShapeDtypeStruct((n_rows, D), F32),
        compiler_params=_cparams(("parallel",)),
        name="proj_residual",
    )(a, w, bias, mod3, x)


def _ffn_kernel(x_ref, sh_ref, sc_ref, gt_ref, g_ref, wg_ref, wu_ref, wd_ref, o_ref, h_scr, acc_scr):
    j = pl.program_id(1)

    @pl.when(j == 0)
    def _():
        h_scr[...] = _modnorm(x_ref[...], g_ref[...], sh_ref[0], sc_ref[0]).astype(BF16)
        acc_scr[...] = jnp.zeros_like(acc_scr)

    h = h_scr[...]
    hid = _silu(_dot(h, wg_ref[...])) * _dot(h, wu_ref[...])
    acc_scr[...] += _dot(hid.astype(BF16), wd_ref[...])

    @pl.when(j == pl.num_programs(1) - 1)
    def _():
        o_ref[...] = x_ref[...] + gt_ref[0] * acc_scr[...]


def _dense_ffn(cfg, x, mod3, g, wg, wu, wd, n_rows):
    D, tm, tf = cfg.D, cfg.TM_FFN, cfg.TF
    ff = wg.shape[1]
    return pl.pallas_call(
        _ffn_kernel,
        grid=(n_rows // tm, ff // tf),
        in_specs=[pl.BlockSpec((tm, D), lambda i, j: (i, 0)),
                  _mod_spec(cfg, tm, 3), _mod_spec(cfg, tm, 4), _mod_spec(cfg, tm, 5),
                  pl.BlockSpec((1, D), lambda i, j: (0, 0)),
                  pl.BlockSpec((D, tf), lambda i, j: (0, j)),
                  pl.BlockSpec((D, tf), lambda i, j: (0, j)),
                  pl.BlockSpec((tf, D), lambda i, j: (j, 0))],
        out_specs=pl.BlockSpec((tm, D), lambda i, j: (i, 0)),
        out_shape=jax.ShapeDtypeStruct((n_rows, D), F32),
        scratch_shapes=[pltpu.VMEM((tm, D), BF16), pltpu.VMEM((tm, D), F32)],
        compiler_params=_cparams(("parallel", "arbitrary")),
        name="dense_ffn",
    )(x, mod3, mod3, mod3, g, wg, wu, wd)


def _router_kernel(x_ref, sh_ref, sc_ref, g_ref, r_ref, h_ref, idx_ref, wt_ref):
    h = _modnorm(x_ref[...], g_ref[...], sh_ref[0], sc_ref[0])
    h_ref[...] = h
    logits = _dot3(h, r_ref[...])
    lane = lax.broadcasted_iota(I32, logits.shape, 1)
    lg = jnp.where(lane < N_EXPERTS, logits, -1e30)
    m1 = jnp.max(lg, axis=-1, keepdims=True)
    i1 = jnp.min(jnp.where(lg == m1, lane, LANES), axis=-1, keepdims=True)
    lg2 = jnp.where(lane == i1, -1e30, lg)
    m2 = jnp.max(lg2, axis=-1, keepdims=True)
    i2 = jnp.min(jnp.where(lg2 == m2, lane, LANES), axis=-1, keepdims=True)
    e = jnp.exp(m2 - m1)
    w1 = 1.0 / (1.0 + e)
    w2 = e / (1.0 + e)
    idx_ref[...] = jnp.where(lane == 0, i1, jnp.where(lane == 1, i2, 0))
    wt_ref[...] = jnp.where(lane == 0, w1, jnp.where(lane == 1, w2, 0.0))


def _router(cfg, x, mod3, g, router_pad, n_rows):
    D, tm = cfg.D, cfg.TM
    return pl.pallas_call(
        _router_kernel,
        grid=(n_rows // tm,),
        in_specs=[pl.BlockSpec((tm, D), lambda i: (i, 0)),
                  _mod_spec(cfg, tm, 3), _mod_spec(cfg, tm, 4),
                  pl.BlockSpec((1, D), lambda i: (0, 0)),
                  pl.BlockSpec((D, LANES), lambda i: (0, 0))],
        out_specs=[pl.BlockSpec((tm, D), lambda i: (i, 0)),
                   pl.BlockSpec((tm, LANES), lambda i: (i, 0)),
                   pl.BlockSpec((tm, LANES), lambda i: (i, 0))],
        out_shape=[jax.ShapeDtypeStruct((n_rows, D), F32),
                   jax.ShapeDtypeStruct((n_rows, LANES), I32),
                   jax.ShapeDtypeStruct((n_rows, LANES), F32)],
        compiler_params=_cparams(("parallel",)),
        name="moe_router",
    )(x, mod3, mod3, g, router_pad)


def _moe_plan(cfg, idx, n_rows):
    ts = cfg.TS
    A = 2 * n_rows
    S = A + N_EXPERTS * ts
    e = idx[:, :2].reshape(A)
    onehot = (e[:, None] == jnp.arange(N_EXPERTS, dtype=I32)[None, :]).astype(I32)
    csum = jnp.cumsum(onehot, axis=0)
    rank = jnp.sum((csum - 1) * onehot, axis=1)
    counts = csum[-1]
    padded = ((counts + ts - 1) // ts) * ts
    ends = jnp.cumsum(padded)
    starts = ends - padded
    slot = starts[e] + rank
    seg_size = jnp.concatenate([padded - counts, (S - ends[-1]).reshape(1)])
    seg_start = jnp.concatenate([starts + counts, ends[-1:]])
    seg_end = jnp.cumsum(seg_size)
    p = jnp.arange(S - A, dtype=I32)
    seg = jnp.sum((p[:, None] >= seg_end[None, :]).astype(I32), axis=1)
    pad_slot = seg_start[seg] + p - (seg_end - seg_size)[seg]
    tile_e = jnp.minimum(jnp.searchsorted(ends, jnp.arange(S // ts, dtype=I32) * ts, side="right"),
                         N_EXPERTS - 1).astype(I32)
    n_used = (ends[-1] // ts).astype(I32).reshape(1)
    return jnp.concatenate([slot, pad_slot]).astype(I32), tile_e, n_used


def _dispatch_kernel(slot_ref, h_ref, xs_hbm, sem, *, n_real):
    per_step = slot_ref.shape[0]
    real = pl.program_id(0) < n_real

    def body(k, carry):
        r = jnp.where(real, k // 2, 0)
        pltpu.make_async_copy(h_ref.at[pl.ds(r, 1)], xs_hbm.at[pl.ds(slot_ref[k], 1)], sem).start()
        return carry

    lax.fori_loop(0, per_step, body, 0, unroll=8)
    for _ in range(2):
        pltpu.make_async_copy(h_ref, xs_hbm.at[pl.ds(0, per_step // 2)], sem).wait()


def _moe_dispatch(cfg, h, all_slot, n_rows):
    D, per_step = cfg.D, cfg.TD
    S = all_slot.shape[0]
    n_real = 2 * n_rows // per_step
    return pl.pallas_call(
        functools.partial(_dispatch_kernel, n_real=n_real),
        grid=(S // per_step,),
        in_specs=[pl.BlockSpec((per_step,), lambda i: (i,), memory_space=pltpu.SMEM),
                  pl.BlockSpec((per_step // 2, D), lambda i: (jnp.minimum(i, n_real - 1), 0))],
        out_specs=pl.BlockSpec(memory_space=pl.ANY),
        out_shape=jax.ShapeDtypeStruct((S, D), F32),
        scratch_shapes=[pltpu.SemaphoreType.DMA(())],
        compiler_params=_cparams(("arbitrary",)),
        name="moe_dispatch",
    )(all_slot, h)


def _expert_up_kernel(te_ref, nu_ref, x_ref, wg_ref, wu_ref, o_ref, wg_scr, wu_scr):
    i = pl.program_id(1)

    @pl.when(jnp.logical_or(i == 0, te_ref[i] != te_ref[jnp.maximum(i - 1, 0)]))
    def _():
        wg_scr[...] = wg_ref[0, 0].astype(BF16)
        wu_scr[...] = wu_ref[0, 0].astype(BF16)

    used = i < nu_ref[0]

    @pl.when(used)
    def _():
        x = x_ref[...].astype(BF16)
        o_ref[...] = (_silu(_dot(x, wg_scr[...])) * _dot(x, wu_scr[...])).astype(BF16)

    @pl.when(jnp.logical_not(used))
    def _():
        o_ref[...] = jnp.zeros_like(o_ref)


def _expert_up(cfg, xs, wg, wu, layer, tile_e, n_used):
    D, ts, tf = cfg.D, cfg.TS, cfg.TFE
    S, ff = xs.shape[0], wg.shape[3]
    return pl.pallas_call(
        _expert_up_kernel,
        grid_spec=pltpu.PrefetchScalarGridSpec(
            num_scalar_prefetch=2, grid=(ff // tf, S // ts),
            in_specs=[pl.BlockSpec((ts, D), lambda j, i, *_: (i, 0)),
                      pl.BlockSpec((1, 1, D, tf), lambda j, i, te, nu: (layer, te[i], 0, j)),
                      pl.BlockSpec((1, 1, D, tf), lambda j, i, te, nu: (layer, te[i], 0, j))],
            out_specs=pl.BlockSpec((ts, tf), lambda j, i, *_: (i, j)),
            scratch_shapes=[pltpu.VMEM((D, tf), BF16), pltpu.VMEM((D, tf), BF16)]),
        out_shape=jax.ShapeDtypeStruct((S, ff), BF16),
        compiler_params=_cparams(("arbitrary", "arbitrary")),
        name="expert_up",
    )(tile_e, n_used, xs, wg, wu)


def _expert_down_kernel(te_ref, nu_ref, h_ref, wd_ref, y_ref, wd_scr):
    i = pl.program_id(1)

    @pl.when(jnp.logical_or(i == 0, te_ref[i] != te_ref[jnp.maximum(i - 1, 0)]))
    def _():
        wd_scr[...] = wd_ref[0, 0].astype(BF16)

    used = i < nu_ref[0]

    @pl.when(used)
    def _():
        y_ref[...] = _dot(h_ref[...], wd_scr[...])

    @pl.when(jnp.logical_not(used))
    def _():
        y_ref[...] = jnp.zeros_like(y_ref)


def _expert_down(cfg, hs, wd, layer, tile_e, n_used):
    D, ts, tn = cfg.D, cfg.TS, cfg.TND
    S, ff = hs.shape
    return pl.pallas_call(
        _expert_down_kernel,
        grid_spec=pltpu.PrefetchScalarGridSpec(
            num_scalar_prefetch=2, grid=(D // tn, S // ts),
            in_specs=[pl.BlockSpec((ts, ff), lambda n, i, *_: (i, 0)),
                      pl.BlockSpec((1, 1, ff, tn), lambda n, i, te, nu: (layer, te[i], 0, n))],
            out_specs=pl.BlockSpec((ts, tn), lambda n, i, *_: (i, n)),
            scratch_shapes=[pltpu.VMEM((ff, tn), BF16)]),
        out_shape=jax.ShapeDtypeStruct((S, D), F32),
        compiler_params=_cparams(("arbitrary", "arbitrary")),
        name="expert_down",
    )(tile_e, n_used, hs, wd)


def _combine_kernel(slot_ref, y_hbm, x_ref, wt_ref, gt_ref, o_ref, ybuf, sem):
    tm = x_ref.shape[0]

    def body(r, carry):
        for k in range(2):
            pltpu.make_async_copy(y_hbm.at[pl.ds(slot_ref[2 * r + k], 1)], ybuf.at[k, pl.ds(r, 1)], sem).start()
        return carry

    lax.fori_loop(0, tm, body, 0, unroll=4)
    for k in range(2):
        pltpu.make_async_copy(y_hbm.at[pl.ds(0, tm)], ybuf.at[k], sem).wait()
    wt = wt_ref[...]
    moe = wt[:, 0:1] * ybuf[0] + wt[:, 1:2] * ybuf[1]
    o_ref[...] = x_ref[...] + gt_ref[0] * moe


def _moe_combine(cfg, x, y, slot, wt, mod3, n_rows):
    D, tm = cfg.D, cfg.TMC
    return pl.pallas_call(
        _combine_kernel,
        grid=(n_rows // tm,),
        in_specs=[pl.BlockSpec((2 * tm,), lambda i: (i,), memory_space=pltpu.SMEM),
                  pl.BlockSpec(memory_space=pl.ANY),
                  pl.BlockSpec((tm, D), lambda i: (i, 0)),
                  pl.BlockSpec((tm, LANES), lambda i: (i, 0)),
                  _mod_spec(cfg, tm, 5)],
        out_specs=pl.BlockSpec((tm, D), lambda i: (i, 0)),
        out_shape=jax.ShapeDtypeStruct((n_rows, D), F32),
        scratch_shapes=[pltpu.VMEM((2, tm, D), F32), pltpu.SemaphoreType.DMA(())],
        compiler_params=_cparams(("arbitrary",)),
        name="moe_combine",
    )(slot, y, x, wt, mod3)


def _moe_layer(cfg, x, mod3, g, router_pad, wg, wu, wd, layer, n_rows):
    h, idx, wt = _router(cfg, x, mod3, g, router_pad, n_rows)
    all_slot, tile_e, n_used = _moe_plan(cfg, idx, n_rows)
    xs = _moe_dispatch(cfg, h, all_slot, n_rows)
    hs = _expert_up(cfg, xs, wg, wu, layer, tile_e, n_used)
    y = _expert_down(cfg, hs, wd, layer, tile_e, n_used)
    return _moe_combine(cfg, x, y, all_slot[:2 * n_rows], wt, mod3, n_rows)


def _dft_matrix(n_half):
    n = 2 * n_half
    nb = 64
    r = jnp.arange(n, dtype=I32)
    is_cos = (r <= n_half)[:, None]
    k = jnp.where(r <= n_half, r, r - n_half)[:, None]
    ang_a = ((k * (jnp.arange(n_half // nb, dtype=I32) * nb)[None, :]) % n).astype(F32) * (2.0 * np.pi / n)
    ang_b = ((k * jnp.arange(nb, dtype=I32)[None, :]) % n).astype(F32) * (2.0 * np.pi / n)
    ca = jnp.where(is_cos, jnp.cos(ang_a), jnp.sin(ang_a))[:, :, None]
    sa = jnp.where(is_cos, jnp.sin(ang_a), -jnp.cos(ang_a))[:, :, None]
    cb, sb = jnp.cos(ang_b)[:, None, :], jnp.sin(ang_b)[:, None, :]
    return (ca * cb - sa * sb).reshape(n, n_half)


def _hy_filter_kernel(z_ref, w1_ref, b1_ref, w2_ref, b2_ref, w3f_ref, w3b_ref, f_ref, df_ref, db_ref,
                      ft_ref, fb_ref, tp_ref, ga_ref, gb_ref, gc_ref, sum_scr, dif_scr, nrm_scr, nyq_scr, *, seq):
    i = pl.program_id(1)
    tm = ft_ref.shape[0]
    n = 2 * seq

    @pl.when(i == 0)
    def _():
        z = z_ref[...]
        a = jnp.sin(f_ref[0:1, :] * (_dot3(z, w1_ref[...]) + b1_ref[...]))
        a = jnp.sin(f_ref[1:2, :] * (_dot3(a, w2_ref[...]) + b2_ref[...]))
        t_unit = z[:, 0:1]
        row = lax.broadcasted_iota(I32, (seq, 1), 0)
        hf = _dot3(a, w3f_ref[...]) * jnp.exp(-t_unit * jnp.exp(df_ref[...]))
        hb = _dot3(a, w3b_ref[...]) * jnp.exp(-t_unit * jnp.exp(db_ref[...]))
        hb = jnp.where(row == 0, 0.0, hb)
        nrm_scr[...] = jnp.sum(jnp.abs(hf) + jnp.abs(hb), axis=0, keepdims=True)
        tot = hf + hb
        sum_scr[...] = tot.astype(BF16)
        dif_scr[...] = (hf - hb).astype(BF16)
        nyq_scr[...] = jnp.sum(jnp.where((row & 1) == 0, tot, -tot), axis=0, keepdims=True)

    p = _dot(ft_ref[...], sum_scr[...])
    q = _dot(fb_ref[...], dif_scr[...])
    inv = 1.0 / nrm_scr[...]
    first = (i * tm + lax.broadcasted_iota(I32, (tm, 1), 0)) == 0
    w_one, w_two = inv * (1.0 / n), inv * (2.0 / n)
    wt = jnp.where(first, w_one, w_two)
    tp_ref[...] = wt * p
    ga_ref[...] = jnp.where(first, 0.0, -(w_two * q))
    gb_ref[...] = jnp.where(first, 0.0, w_two * q)
    gc_ref[...] = jnp.where(first, w_one * nyq_scr[...], w_two * p)


def _pad_to(a, shape):
    return jnp.zeros(shape, a.dtype).at[tuple(slice(0, d) for d in a.shape)].set(a)


def _hy_filter(cfg, seq, z, w1, b1, w2, b2, w3, sin_freq, log_decay, fmat):
    D, tc = cfg.D, cfg.TC
    tm = min(cfg.TMF, seq)
    pw = LANES
    full = lambda shp: pl.BlockSpec(shp, lambda j, i: (0,) * len(shp))
    col = lambda rows, off: pl.BlockSpec((rows, tc), lambda j, i: (0, off + j))
    out_spec = pl.BlockSpec((tm, tc), lambda j, i: (i, j))
    w3p = _pad_to(w3, (pw, 2 * D))
    return pl.pallas_call(
        functools.partial(_hy_filter_kernel, seq=seq),
        grid=(D // tc, seq // tm),
        in_specs=[full((seq, pw)), full((pw, pw)), full((1, pw)), full((pw, pw)), full((1, pw)),
                  col(pw, 0), col(pw, D // tc), full((2, pw)), col(1, 0), col(1, D // tc),
                  pl.BlockSpec((tm, seq), lambda j, i: (i, 0)),
                  pl.BlockSpec((tm, seq), lambda j, i: (seq // tm + i, 0))],
        out_specs=[out_spec] * 4,
        out_shape=[jax.ShapeDtypeStruct((seq, D), F32)] * 4,
        scratch_shapes=[pltpu.VMEM((seq, tc), BF16), pltpu.VMEM((seq, tc), BF16),
                        pltpu.VMEM((1, tc), F32), pltpu.VMEM((1, tc), F32)],
        compiler_params=_cparams(("arbitrary", "arbitrary")),
        name="hyena_filter",
    )(_pad_to(z, (seq, pw)), _pad_to(w1, (pw, pw)), _pad_to(b1.reshape(1, -1), (1, pw)),
      _pad_to(w2, (pw, pw)), _pad_to(b2.reshape(1, -1), (1, pw)), w3p, w3p, _pad_to(sin_freq, (2, pw)),
      log_decay.reshape(1, 2 * D), log_decay.reshape(1, 2 * D), fmat, fmat)


def _hy_in_kernel(x_ref, sh_ref, sc_ref, g_ref, w0_ref, w1_ref, w2_ref, b_ref, cw_ref, cb_ref,
                  x0_ref, u_ref, ub_ref, h_scr):
    j = pl.program_id(1)
    seq = x_ref.shape[0]

    @pl.when(j == 0)
    def _():
        h_scr[...] = _modnorm(x_ref[...], g_ref[...], sh_ref[0], sc_ref[0]).astype(BF16)

    h = h_scr[...]
    row = lax.broadcasted_iota(I32, (seq, 1), 0)

    def branch(w_ref, t):
        p = _dot(h, w_ref[...]) + b_ref[t]
        prev = jnp.where(row == 0, 0.0, pltpu.roll(p, 1, 0))
        nxt = jnp.where(row == seq - 1, 0.0, pltpu.roll(p, seq - 1, 0))
        cw = cw_ref[t]
        return cw[0:1] * prev + cw[1:2] * p + cw[2:3] * nxt + cb_ref[t]

    x0_ref[...] = branch(w0_ref, 0)
    u = branch(w1_ref, 1) * branch(w2_ref, 2)
    u_ref[...] = u
    ub_ref[...] = u.astype(BF16)


def _hy_in(cfg, seq, row_blk0, x, mod3, g, w, b3, cw3, cb3):
    D, tc = cfg.D, cfg.TC
    nj = D // tc
    rows = cfg.B * seq
    mrow = (lambda b: b) if row_blk0 == 0 else (lambda b: cfg.B)
    mod = lambda k: pl.BlockSpec((1, 1, D), lambda b, j: (mrow(b) * 6 + k, 0, 0))
    wspec = lambda t: pl.BlockSpec((D, tc), lambda b, j: (0, t * nj + j))
    out = pl.BlockSpec((seq, tc), lambda b, j: (b, j))
    return pl.pallas_call(
        _hy_in_kernel,
        grid=(cfg.B, nj),
        in_specs=[pl.BlockSpec((seq, D), lambda b, j: (row_blk0 + b, 0)), mod(0), mod(1),
                  pl.BlockSpec((1, D), lambda b, j: (0, 0)),
                  wspec(0), wspec(1), wspec(2),
                  pl.BlockSpec((3, 1, tc), lambda b, j: (0, 0, j)),
                  pl.BlockSpec((3, 3, tc), lambda b, j: (0, 0, j)),
                  pl.BlockSpec((3, 1, tc), lambda b, j: (0, 0, j))],
        out_specs=[out, out, out],
        out_shape=[jax.ShapeDtypeStruct((rows, D), F32), jax.ShapeDtypeStruct((rows, D), F32),
                   jax.ShapeDtypeStruct((rows, D), BF16)],
        scratch_shapes=[pltpu.VMEM((seq, D), BF16)],
        compiler_params=_cparams(("parallel", "arbitrary")),
        name="hyena_in",
    )(x, mod3, mod3, g, w, w, w, b3, cw3, cb3)


def _hy_fwd_kernel(ft_ref, fb_ref, u_ref, tp_ref, ga_ref, gb_ref, gc_ref, zt_ref, zb_ref):
    u = u_ref[...]
    a = _dot(ft_ref[...], u)
    b = _dot(fb_ref[...], u)
    zt_ref[0] = (a * tp_ref[...] + b * ga_ref[...]).astype(BF16)
    zb_ref[0] = (a * gb_ref[...] + b * gc_ref[...]).astype(BF16)


def _hy_fwd(cfg, seq, ub, fmat, filt):
    D, tc = cfg.D, cfg.TC
    tm = min(cfg.TMF, seq)
    nt = seq // tm
    g_spec = pl.BlockSpec((tm, tc), lambda b, j, i: (i, j))
    return pl.pallas_call(
        _hy_fwd_kernel,
        grid=(cfg.B, D // tc, nt),
        in_specs=[pl.BlockSpec((tm, seq), lambda b, j, i: (i, 0)),
                  pl.BlockSpec((tm, seq), lambda b, j, i: (nt + i, 0)),
                  pl.BlockSpec((seq, tc), lambda b, j, i: (b, j)),
                  g_spec, g_spec, g_spec, g_spec],
        out_specs=[pl.BlockSpec((1, tm, tc), lambda b, j, i: (b, i, j))] * 2,
        out_shape=[jax.ShapeDtypeStruct((cfg.B, seq, D), BF16)] * 2,
        compiler_params=_cparams(("parallel", "parallel", "arbitrary")),
        name="hyena_fwd_dft",
    )(fmat, fmat, ub, *filt)


def _hy_inv_kernel(fi_ref, zt_ref, zb_ref, u_ref, x0_ref, skip_ref, y_ref, *, seq):
    fi = fi_ref[...]
    conv = _dot(fi[:, :seq], zt_ref[0]) + _dot(fi[:, seq:], zb_ref[0])
    v = conv + skip_ref[...] * u_ref[...]
    y_ref[...] = (x0_ref[...] * v).astype(BF16)


def _hy_inv(cfg, seq, zt, zb, finv, u, x0, skip):
    D, tc = cfg.D, cfg.TC
    tm = min(cfg.TMF, seq)
    io = pl.BlockSpec((tm, tc), lambda b, j, i: (b * (seq // tm) + i, j))
    z_spec = pl.BlockSpec((1, seq, tc), lambda b, j, i: (b, 0, j))
    return pl.pallas_call(
        functools.partial(_hy_inv_kernel, seq=seq),
        grid=(cfg.B, D // tc, seq // tm),
        in_specs=[pl.BlockSpec((tm, 2 * seq), lambda b, j, i: (i, 0)), z_spec, z_spec, io, io,
                  pl.BlockSpec((1, tc), lambda b, j, i: (0, j))],
        out_specs=io,
        out_shape=jax.ShapeDtypeStruct((cfg.B * seq, D), BF16),
        compiler_params=_cparams(("parallel", "parallel", "arbitrary")),
        name="hyena_inv_dft",
    )(finv, zt, zb, u, x0, skip)


def _hy_positions(seq):
    t = jnp.arange(seq, dtype=F32)
    t_unit = t / max(seq - 1, 1)
    phase = 2.0 * np.pi * t / seq
    bands = jnp.linspace(1e-4, HY_BANDS - 1, HY_BANDS, dtype=F32)
    ang = phase[:, None] * bands[None, :]
    return jnp.concatenate([t_unit[:, None], jnp.cos(ang), -jnp.sin(ang)], axis=-1)


def _hyena_seq(cfg, seq, row_blk0, x, mod3, g, w_in, b3, cw3, cb3, filt_params, skip):
    fmat = _dft_matrix(seq)
    fb16 = fmat.astype(BF16)
    filt = _hy_filter(cfg, seq, _hy_positions(seq), *filt_params, fb16)
    x0, u, ub = _hy_in(cfg, seq, row_blk0, x, mod3, g, w_in, b3, cw3, cb3)
    zt, zb = _hy_fwd(cfg, seq, ub, fb16, filt)
    return _hy_inv(cfg, seq, zt, zb, fmat.T.astype(BF16), u, x0, skip)


def _final_norm_kernel(x_ref, g_ref, o_ref):
    x = x_ref[...]
    o_ref[...] = (x * lax.rsqrt(jnp.mean(x * x, axis=-1, keepdims=True) + NORM_EPS)) * g_ref[...]


def _final_norm(cfg, x, g):
    D, tm = cfg.D, cfg.TM
    return pl.pallas_call(
        _final_norm_kernel,
        grid=(cfg.ML // tm,),
        in_specs=[pl.BlockSpec((tm, D), lambda i: (i, 0)), pl.BlockSpec((1, D), lambda i: (0, 0))],
        out_specs=pl.BlockSpec((tm, D), lambda i: (i, 0)),
        out_shape=jax.ShapeDtypeStruct((cfg.ML, D), F32),
        compiler_params=_cparams(("parallel",)),
        name="final_norm",
    )(x, g)


def _rope_tables(cfg):
    L, tm = cfg.L, cfg.TM
    t = jnp.arange(L, dtype=I32)
    row, col = (t // cfg.GRID_W).astype(F32), (t % cfg.GRID_W).astype(F32)
    inv = ROPE_BASE ** (-jnp.arange(ROPE_FREQS, dtype=F32) / ROPE_FREQS)
    a0, a1 = row[:, None] * inv, col[:, None] * inv
    cos_h = jnp.concatenate([jnp.cos(a0), jnp.cos(a0), jnp.cos(a1), jnp.cos(a1)], axis=1)
    sin_h = jnp.concatenate([-jnp.sin(a0), jnp.sin(a0), -jnp.sin(a1), jnp.sin(a1)], axis=1)
    cos_t = jnp.concatenate([jnp.tile(cos_h, (1, 2)), jnp.ones((tm, LANES), F32)], axis=0)
    sin_t = jnp.concatenate([jnp.tile(sin_h, (1, 2)), jnp.zeros((tm, LANES), F32)], axis=0)
    return cos_t, sin_t


def _head_perm(D):
    heads = []
    for hp in range(D // (2 * GROUP * HEAD_DIM)):
        for c in range(GROUP):
            heads += [2 * GROUP * hp + c, 2 * GROUP * hp + GROUP + c]
    return np.concatenate([np.arange(h * HEAD_DIM, (h + 1) * HEAD_DIM) for h in heads])


def _attn_layer(cfg, mode, last, x, mod3, g, w_in, w_out, q_g, k_g, sink, tables):
    D = cfg.D
    perm = _head_perm(D)
    w = jnp.concatenate([w_in[:, :D][:, perm], w_in[:, D:]], axis=1).astype(BF16)
    wo = w_out[perm, :].astype(BF16)
    qg = jnp.tile(q_g * (HEAD_DIM ** -0.5), 2).reshape(1, LANES)
    kg = jnp.tile(k_g, 2).reshape(1, LANES)
    blk = np.kron(np.eye(2, dtype=np.float32), np.ones((HEAD_DIM, HEAD_DIM), np.float32))
    cos_t, sin_t = tables
    q, k, v = _qkv_proj(cfg, x, mod3, g, w, qg, kg, cos_t, sin_t, jnp.asarray(blk, BF16))
    use_sink = sink is not None
    sink = sink.astype(F32) if use_sink else jnp.zeros((D // HEAD_DIM,), F32)
    o = _attention(cfg, q, k, v, sink, mode, use_sink, not last)
    n_rows = cfg.ML if last else cfg.MT
    return _proj_residual(cfg, o, wo, jnp.zeros((1, D), F32), mod3, 2, x, n_rows)


def _hyena_layer(cfg, last, x, mod3, g, w_in, b_in, conv_w, conv_b, filt_params, skip, w_out, b_out):
    D = cfg.D
    w = w_in.astype(BF16)
    b3, cb3 = b_in.reshape(3, 1, D), conv_b.reshape(3, 1, D)
    cw3 = conv_w.reshape(3, 3, D).transpose(1, 0, 2)
    skip = skip.reshape(1, D)
    y = _hyena_seq(cfg, cfg.L, 0, x, mod3, g, w, b3, cw3, cb3, filt_params, skip)
    n_rows = cfg.ML
    if not last:
        yc = _hyena_seq(cfg, cfg.LC, cfg.ML // cfg.LC, x, mod3, g, w, b3, cw3, cb3, filt_params, skip)
        y = jnp.concatenate([y, yc], axis=0)
        n_rows = cfg.MT
    return _proj_residual(cfg, y, w_out.astype(BF16), b_out.reshape(1, D), mod3, 2, x, n_rows)


def _forward(cfg, x, c, ctx, c_ctx, mod_w, mod_b, norm_g,
             gattn_w_in, gattn_w_out, gattn_q_norm, gattn_k_norm,
             wattn_w_in, wattn_w_out, wattn_q_norm, wattn_k_norm, wattn_sink,
             hy_w_in, hy_b_in, hy_conv_w, hy_conv_b, hy_ffn_w1, hy_ffn_b1, hy_ffn_w2, hy_ffn_b2,
             hy_ffn_w3, hy_sin_freq, hy_log_decay, hy_skip, hy_w_out, hy_b_out,
             ffn_w_gate, ffn_w_up, ffn_w_down,
             moe_router, moe_w_gate, moe_w_up, moe_w_down, final_norm_g):
    D, depth = cfg.D, cfg.DEPTH
    xs = jnp.concatenate([x.reshape(cfg.ML, D), ctx.reshape(cfg.MC, D)], axis=0)
    mod = _mod_vectors(cfg, c, c_ctx, mod_w, mod_b)
    tables = _rope_tables(cfg)
    for i in range(depth):
        last = i == depth - 1
        mod3 = mod[i]
        kind, j = i % 3, i // 3
        g_a, g_f = norm_g[i, 0].reshape(1, D), norm_g[i, 1].reshape(1, D)
        if kind == 0:
            xs = _attn_layer(cfg, "global", last, xs, mod3, g_a, gattn_w_in[j], gattn_w_out[j],
                             gattn_q_norm[j], gattn_k_norm[j], None, tables)
        elif kind == 1:
            xs = _attn_layer(cfg, "window", last, xs, mod3, g_a, wattn_w_in[j], wattn_w_out[j],
                             wattn_q_norm[j], wattn_k_norm[j], wattn_sink[j], tables)
        else:
            filt = (hy_ffn_w1[j], hy_ffn_b1[j], hy_ffn_w2[j], hy_ffn_b2[j], hy_ffn_w3[j],
                    hy_sin_freq[j], hy_log_decay[j])
            xs = _hyena_layer(cfg, last, xs, mod3, g_a, hy_w_in[j], hy_b_in[j], hy_conv_w[j], hy_conv_b[j],
                              filt, hy_skip[j], hy_w_out[j], hy_b_out[j])
        n_rows = cfg.ML if last else cfg.MT
        k_ff = i // 2
        if i % 2 == 0:
            xs = _dense_ffn(cfg, xs, mod3, g_f, ffn_w_gate[k_ff].astype(BF16), ffn_w_up[k_ff].astype(BF16),
                            ffn_w_down[k_ff].astype(BF16), n_rows)
        else:
            router_pad = jnp.zeros((D, LANES), F32).at[:, :N_EXPERTS].set(moe_router[k_ff])
            xs = _moe_layer(cfg, xs, mod3, g_f, router_pad, moe_w_gate, moe_w_up, moe_w_down, k_ff, n_rows)
    return _final_norm(cfg, xs, final_norm_g.reshape(1, D)).reshape(cfg.B, cfg.L, D)


def kernel(x, c, ctx, c_ctx, mod_w, mod_b, norm_g, gattn_w_in, gattn_w_out, gattn_q_norm, gattn_k_norm, wattn_w_in, wattn_w_out, wattn_q_norm, wattn_k_norm, wattn_sink, hy_w_in, hy_b_in, hy_conv_w, hy_conv_b, hy_ffn_w1, hy_ffn_b1, hy_ffn_w2, hy_ffn_b2, hy_ffn_w3, hy_sin_freq, hy_log_decay, hy_skip, hy_w_out, hy_b_out, ffn_w_gate, ffn_w_up, ffn_w_down, moe_router, moe_w_gate, moe_w_up, moe_w_down, final_norm_g):
    return _forward(Cfg(), x, c, ctx, c_ctx, mod_w, mod_b, norm_g, gattn_w_in, gattn_w_out, gattn_q_norm,
                    gattn_k_norm, wattn_w_in, wattn_w_out, wattn_q_norm, wattn_k_norm, wattn_sink, hy_w_in,
                    hy_b_in, hy_conv_w, hy_conv_b, hy_ffn_w1, hy_ffn_b1, hy_ffn_w2, hy_ffn_b2, hy_ffn_w3,
                    hy_sin_freq, hy_log_decay, hy_skip, hy_w_out, hy_b_out, ffn_w_gate, ffn_w_up, ffn_w_down,
                    moe_router, moe_w_gate, moe_w_up, moe_w_down, final_norm_g)
```
